```python
import jax, jax.numpy as jnp
from jax import lax
import numpy as np

D_MODEL = 4096
BATCH = 4
SEQ = 2048
DEPTH = 4
DEC_BATCH = 32
DEC_SEQ = 1
PAST_LEN = 8192
PAGE_SIZE = 128

N_HEADS = 64
N_KV_HEADS = 8
HEAD_DIM = 64
GROUP = N_HEADS // N_KV_HEADS
QKV_DIM = (N_HEADS + 2 * N_KV_HEADS) * HEAD_DIM
WINDOW = 128
BLOCK = WINDOW
ATTN_SCALE = HEAD_DIM ** -0.5
CONV_WIDTH = 3
D_FF = 256 * ((8 * D_MODEL // 3 + 255) // 256)
PLE_DIM = 256
N_MIXERS = 2
N_ATTN_LAYERS = (DEPTH + 1) // 2
N_CONV_LAYERS = DEPTH // 2
NORM_EPS = 1e-6
NEG_INF = -1e30

kernel_name = 'hybrid_swa_sink_shortconv_macaron_step'


def _rmsnorm(x, g):
    xf = x.astype(jnp.float32)
    y = xf * lax.rsqrt(jnp.mean(xf * xf, axis=-1, keepdims=True) + NORM_EPS) * g.astype(jnp.float32)
    return y.astype(x.dtype)


def _half_ffn(x, g, w_gate, w_up, w_down):
    h = _rmsnorm(x, g)
    return x + 0.5 * ((jax.nn.silu(h @ w_gate) * (h @ w_up)) @ w_down)


def _ple(x, p, g, w_proj, w_gate):
    gate = jax.nn.sigmoid(_rmsnorm(x, g) @ w_gate)
    return x + gate * (p.astype(x.dtype) @ w_proj)


def _qkv(h, w_qkv, b_qkv):
    n, t, _ = h.shape
    z = h @ w_qkv + b_qkv
    q = z[..., :N_HEADS * HEAD_DIM].reshape(n, t, N_KV_HEADS, GROUP, HEAD_DIM)
    k = z[..., N_HEADS * HEAD_DIM:(N_HEADS + N_KV_HEADS) * HEAD_DIM].reshape(n, t, N_KV_HEADS, HEAD_DIM)
    v = z[..., (N_HEADS + N_KV_HEADS) * HEAD_DIM:].reshape(n, t, N_KV_HEADS, HEAD_DIM)
    return q, k, v


def _window_sink_attend(q, k, v, q_pos, k_pos, sinks):
    s = jnp.einsum('...qngd,...knd->...ngqk', q.astype(jnp.float32), k.astype(jnp.float32)) * ATTN_SCALE
    delta = q_pos[..., :, None] - k_pos[..., None, :]
    valid = (delta >= 0) & (delta <= WINDOW) & (k_pos[..., None, :] >= 0)
    s = jnp.where(valid[..., None, None, :, :], s, NEG_INF)
    sink = sinks.astype(jnp.float32).reshape(N_KV_HEADS, GROUP, 1, 1)
    m = jnp.maximum(jnp.max(s, axis=-1, keepdims=True), sink)
    pr = jnp.exp(s - m)
    pr = pr / (jnp.sum(pr, axis=-1, keepdims=True) + jnp.exp(sink - m))
    o = jnp.einsum('...ngqk,...knd->...qngd', pr, v.astype(jnp.float32))
    return o.astype(v.dtype)


def _attn_prompt(h, w_qkv, b_qkv, sinks, w_o, b_o):
    n, t, _ = h.shape
    q, k, v = _qkv(h, w_qkv, b_qkv)
    nb = t // BLOCK
    qb = q.reshape(n, nb, BLOCK, N_KV_HEADS, GROUP, HEAD_DIM)
    pad = jnp.zeros((n, BLOCK, N_KV_HEADS, HEAD_DIM), k.dtype)
    kp = jnp.concatenate([pad, k], axis=1).reshape(n, nb + 1, BLOCK, N_KV_HEADS, HEAD_DIM)
    vp = jnp.concatenate([pad, v], axis=1).reshape(n, nb + 1, BLOCK, N_KV_HEADS, HEAD_DIM)
    kb = jnp.concatenate([kp[:, :-1], kp[:, 1:]], axis=2)
    vb = jnp.concatenate([vp[:, :-1], vp[:, 1:]], axis=2)
    pos = jnp.arange(-BLOCK, t).reshape(nb + 1, BLOCK)
    k_pos = jnp.concatenate([pos[:-1], pos[1:]], axis=1)
    q_pos = pos[1:]
    o = _window_sink_attend(qb, kb, vb, q_pos, k_pos, sinks)
    y = o.reshape(n, t, N_HEADS * HEAD_DIM) @ w_o + b_o
    wp = min(WINDOW, t)
    return y, k[:, t - wp:], v[:, t - wp:]


def _attn_sample(h, ck, cv, w_qkv, b_qkv, sinks, w_o, b_o):
    n, t, _ = h.shape
    wb = ck.shape[1]
    q, k, v = _qkv(h, w_qkv, b_qkv)
    kk = jnp.concatenate([ck.astype(k.dtype), k], axis=1)
    vv = jnp.concatenate([cv.astype(v.dtype), v], axis=1)
    q_pos = PAST_LEN + jnp.arange(t)
    k_pos = jnp.concatenate([PAST_LEN - wb + jnp.arange(wb), q_pos])
    o = _window_sink_attend(q, kk, vv, q_pos, k_pos, sinks)
    y = o.reshape(n, t, N_HEADS * HEAD_DIM) @ w_o + b_o
    return y, kk[:, -wb:], vv[:, -wb:]


def _conv_gates(h, w_in):
    z = h @ w_in
    gb = z[..., :D_MODEL]
    gc = z[..., D_MODEL:2 * D_MODEL]
    u = z[..., 2 * D_MODEL:]
    return gb, gc * u


def _dwconv(u_pad, w):
    L = u_pad.shape[1] - (CONV_WIDTH - 1)
    out = w[0] * u_pad[:, 0:L]
    for j in range(1, CONV_WIDTH):
        out = out + w[j] * u_pad[:, j:j + L]
    return out


def _conv_prompt(h, w_in, w_conv, w_out):
    n = h.shape[0]
    gb, cu = _conv_gates(h, w_in)
    u_pad = jnp.concatenate([jnp.zeros((n, CONV_WIDTH - 1, D_MODEL), cu.dtype), cu], axis=1)
    y = (gb * _dwconv(u_pad, w_conv)) @ w_out
    return y, u_pad[:, -(CONV_WIDTH - 1):]


def _conv_sample(h, st, w_in, w_conv, w_out):
    gb, cu = _conv_gates(h, w_in)
    u_pad = jnp.concatenate([st.astype(cu.dtype), cu], axis=1)
    y = (gb * _dwconv(u_pad, w_conv)) @ w_out
    return y, u_pad[:, -(CONV_WIDTH - 1):]


def setup_inputs(seed: int = 0) -> dict:
    key = jax.random.key(seed)
    ks = jax.random.split(key, 24)

    def nrm(k, shape, scale):
        return jax.random.normal(k, shape, jnp.float32) * scale

    w_buf = min(WINDOW, PAST_LEN)
    return {
        'x_prompt': nrm(ks[0], (BATCH, SEQ, D_MODEL), 1.0),
        'x_sample': nrm(ks[1], (DEC_BATCH, DEC_SEQ, D_MODEL), 1.0),
        'cache_k': nrm(ks[2], (N_ATTN_LAYERS, DEC_BATCH, w_buf, N_KV_HEADS, HEAD_DIM), 1.0),
        'cache_v': nrm(ks[3], (N_ATTN_LAYERS, DEC_BATCH, w_buf, N_KV_HEADS, HEAD_DIM), 1.0),
        'state_conv': nrm(ks[4], (N_CONV_LAYERS, DEC_BATCH, CONV_WIDTH - 1, D_MODEL), 1.0),
        'p_prompt': nrm(ks[5], (DEPTH, BATCH, SEQ, PLE_DIM), 1.0),
        'p_sample': nrm(ks[6], (DEPTH, DEC_BATCH, DEC_SEQ, PLE_DIM), 1.0),
        'norm_gains': 1.0 + nrm(ks[7], (DEPTH, 4, D_MODEL), 0.05),
        'final_norm_gain': 1.0 + nrm(ks[8], (D_MODEL,), 0.05),
        'w_ffn_gate': nrm(ks[9], (DEPTH, 2, D_MODEL, D_FF), D_MODEL ** -0.5),
        'w_ffn_up': nrm(ks[10], (DEPTH, 2, D_MODEL, D_FF), D_MODEL ** -0.5),
        'w_ffn_down': nrm(ks[11], (DEPTH, 2, D_FF, D_MODEL), D_FF ** -0.5),
        'w_qkv': nrm(ks[12], (N_ATTN_LAYERS, D_MODEL, QKV_DIM), D_MODEL ** -0.5),
        'b_qkv': nrm(ks[13], (N_ATTN_LAYERS, QKV_DIM), 0.02),
        'attn_sinks': nrm(ks[14], (N_ATTN_LAYERS, N_HEADS), 0.5),
        'w_o': nrm(ks[15], (N_ATTN_LAYERS, N_HEADS * HEAD_DIM, D_MODEL), (N_HEADS * HEAD_DIM) ** -0.5),
        'b_o': nrm(ks[16], (N_ATTN_LAYERS, D_MODEL), 0.02),
        'w_conv_in': nrm(ks[17], (N_CONV_LAYERS, D_MODEL, 3 * D_MODEL), D_MODEL ** -0.5),
        'conv_w': nrm(ks[18], (N_CONV_LAYERS, CONV_WIDTH, D_MODEL), CONV_WIDTH ** -0.5),
        'w_conv_out': nrm(ks[19], (N_CONV_LAYERS, D_MODEL, D_MODEL), D_MODEL ** -0.5),
        'w_ple_proj': nrm(ks[20], (DEPTH, PLE_DIM, D_MODEL), PLE_DIM ** -0.5),
        'w_ple_gate': nrm(ks[21], (DEPTH, D_MODEL, D_MODEL), D_MODEL ** -0.5),
    }


def reference(x_prompt, x_sample, cache_k, cache_v, state_conv, p_prompt, p_sample,
              norm_gains, final_norm_gain, w_ffn_gate, w_ffn_up, w_ffn_down,
              w_qkv, b_qkv, attn_sinks, w_o, b_o, w_conv_in, conv_w, w_conv_out,
              w_ple_proj, w_ple_gate):
    xp, xs = x_prompt, x_sample
    nkp, nvp, nks, nvs, ncp, ncs = [], [], [], [], [], []
    for i in range(DEPTH):
        g = norm_gains[i]
        xp = _half_ffn(xp, g[0], w_ffn_gate[i, 0], w_ffn_up[i, 0], w_ffn_down[i, 0])
        xs = _half_ffn(xs, g[0], w_ffn_gate[i, 0], w_ffn_up[i, 0], w_ffn_down[i, 0])
        hp = _rmsnorm(xp, g[1])
        hs = _rmsnorm(xs, g[1])
        j = i // N_MIXERS
        if i % N_MIXERS == 0:
            op, kp, vp = _attn_prompt(hp, w_qkv[j], b_qkv[j], attn_sinks[j], w_o[j], b_o[j])
            os_, ks_, vs_ = _attn_sample(hs, cache_k[j], cache_v[j], w_qkv[j], b_qkv[j], attn_sinks[j], w_o[j], b_o[j])
            nkp.append(kp); nvp.append(vp); nks.append(ks_); nvs.append(vs_)
        else:
            op, cp = _conv_prompt(hp, w_conv_in[j], conv_w[j], w_conv_out[j])
            os_, cs = _conv_sample(hs, state_conv[j], w_conv_in[j], conv_w[j], w_conv_out[j])
            ncp.append(cp); ncs.append(cs)
        xp = xp + op
        xs = xs + os_
        xp = _half_ffn(xp, g[2], w_ffn_gate[i, 1], w_ffn_up[i, 1], w_ffn_down[i, 1])
        xs = _half_ffn(xs, g[2], w_ffn_gate[i, 1], w_ffn_up[i, 1], w_ffn_down[i, 1])
        xp = _ple(xp, p_prompt[i], g[3], w_ple_proj[i], w_ple_gate[i])
        xs = _ple(xs, p_sample[i], g[3], w_ple_proj[i], w_ple_gate[i])
    y_prompt = _rmsnorm(xp, final_norm_gain)
    y_sample = _rmsnorm(xs, final_norm_gain)
    return (y_prompt, y_sample, jnp.stack(nkp), jnp.stack(nvp), jnp.stack(nks), jnp.stack(nvs), jnp.stack(ncp), jnp.stack(ncs))
```

```python
import functools

import jax
import jax.numpy as jnp
from jax import lax
from jax.experimental import pallas as pl
from jax.experimental.pallas import tpu as pltpu

NORM_EPS = 1e-6
NEG_INF = -1e30
HEAD_DIM = 64
WINDOW = 128
CONV_WIDTH = 3
PAST_LEN = 8192
FFN_PAD = 512

V7X_VMEM_LIMIT_BYTES = 56 * 1024 * 1024

BF16 = jnp.bfloat16
F32 = jnp.float32


def _params(n_axes):
    return pltpu.CompilerParams(
        dimension_semantics=("arbitrary",) * n_axes,
        vmem_limit_bytes=V7X_VMEM_LIMIT_BYTES,
    )


def _tile(dim, target):
    if dim <= target:
        return dim
    t = target
    while dim % t:
        t //= 2
    return t


def _rmsnorm_bf16(x, g):
    y = x * lax.rsqrt(jnp.mean(x * x, axis=-1, keepdims=True) + NORM_EPS) * g
    return y.astype(BF16)


def _dot(a, b):
    return jnp.dot(a, b, preferred_element_type=F32)


def _swiglu_up_kernel(x_ref, g_ref, wg_ref, wu_ref, o_ref, h_ref):
    @pl.when(pl.program_id(1) == 0)
    def _():
        h_ref[...] = _rmsnorm_bf16(x_ref[...], g_ref[...])

    h = h_ref[...]
    gate = _dot(h, wg_ref[...])
    up = _dot(h, wu_ref[...])
    o_ref[...] = (gate * jax.nn.sigmoid(gate) * up).astype(o_ref.dtype)


def swiglu_up(x, g, wg, wu, *, tm, tn):
    m, d = x.shape
    f = wg.shape[1]
    return pl.pallas_call(
        _swiglu_up_kernel,
        grid=(m // tm, f // tn),
        in_specs=[
            pl.BlockSpec((tm, d), lambda i, j: (i, 0)),
            pl.BlockSpec((1, d), lambda i, j: (0, 0)),
            pl.BlockSpec((d, tn), lambda i, j: (0, j)),
            pl.BlockSpec((d, tn), lambda i, j: (0, j)),
        ],
        out_specs=pl.BlockSpec((tm, tn), lambda i, j: (i, j)),
        out_shape=jax.ShapeDtypeStruct((m, f), BF16),
        scratch_shapes=[pltpu.VMEM((tm, d), BF16)],
        compiler_params=_params(2),
        name="swiglu_up",
    )(x, g, wg, wu)


def _norm_linear_kernel(x_ref, g_ref, w_ref, b_ref, o_ref, h_ref):
    @pl.when(pl.program_id(1) == 0)
    def _():
        h_ref[...] = _rmsnorm_bf16(x_ref[...], g_ref[...])

    o_ref[...] = _dot(h_ref[...], w_ref[...]) + b_ref[...]


def norm_linear(x, g, w, b, *, tm, tn):
    m, d = x.shape
    n = w.shape[1]
    return pl.pallas_call(
        _norm_linear_kernel,
        grid=(m // tm, n // tn),
        in_specs=[
            pl.BlockSpec((tm, d), lambda i, j: (i, 0)),
            pl.BlockSpec((1, d), lambda i, j: (0, 0)),
            pl.BlockSpec((d, tn), lambda i, j: (0, j)),
            pl.BlockSpec((1, tn), lambda i, j: (0, j)),
        ],
        out_specs=pl.BlockSpec((tm, tn), lambda i, j: (i, j)),
        out_shape=jax.ShapeDtypeStruct((m, n), F32),
        scratch_shapes=[pltpu.VMEM((tm, d), BF16)],
        compiler_params=_params(2),
        name="norm_linear",
    )(x, g, w, b)


def _conv_gates_kernel(x_ref, g_ref, wb_ref, wc_ref, wu_ref, gb_ref, cu_ref, h_ref):
    @pl.when(pl.program_id(1) == 0)
    def _():
        h_ref[...] = _rmsnorm_bf16(x_ref[...], g_ref[...])

    h = h_ref[...]
    gb_ref[...] = _dot(h, wb_ref[...])
    cu_ref[...] = _dot(h, wc_ref[...]) * _dot(h, wu_ref[...])


def conv_gates(x, g, w_in, *, tm, tn):
    m, d = x.shape
    nblk = d // tn
    return pl.pallas_call(
        _conv_gates_kernel,
        grid=(m // tm, nblk),
        in_specs=[
            pl.BlockSpec((tm, d), lambda i, j: (i, 0)),
            pl.BlockSpec((1, d), lambda i, j: (0, 0)),
            pl.BlockSpec((d, tn), lambda i, j: (0, j)),
            pl.BlockSpec((d, tn), lambda i, j: (0, j + nblk)),
            pl.BlockSpec((d, tn), lambda i, j: (0, j + 2 * nblk)),
        ],
        out_specs=[
            pl.BlockSpec((tm, tn), lambda i, j: (i, j)),
            pl.BlockSpec((tm, tn), lambda i, j: (i, j)),
        ],
        out_shape=[jax.ShapeDtypeStruct((m, d), F32), jax.ShapeDtypeStruct((m, d), F32)],
        scratch_shapes=[pltpu.VMEM((tm, d), BF16)],
        compiler_params=_params(2),
        name="conv_gates",
    )(x, g, w_in, w_in, w_in)


def _ple_kernel(xrow_ref, g_ref, wg_ref, x_ref, p_ref, wp_ref, o_ref, h_ref):
    @pl.when(pl.program_id(1) == 0)
    def _():
        h_ref[...] = _rmsnorm_bf16(xrow_ref[...], g_ref[...])

    gate = jax.nn.sigmoid(_dot(h_ref[...], wg_ref[...]))
    proj = _dot(p_ref[...].astype(BF16), wp_ref[...])
    o_ref[...] = x_ref[...] + gate * proj


def ple(x, g, w_gate, p, w_proj, *, tm, tn):
    m, d = x.shape
    pd = p.shape[1]
    return pl.pallas_call(
        _ple_kernel,
        grid=(m // tm, d // tn),
        in_specs=[
            pl.BlockSpec((tm, d), lambda i, j: (i, 0)),
            pl.BlockSpec((1, d), lambda i, j: (0, 0)),
            pl.BlockSpec((d, tn), lambda i, j: (0, j)),
            pl.BlockSpec((tm, tn), lambda i, j: (i, j)),
            pl.BlockSpec((tm, pd), lambda i, j: (i, 0)),
            pl.BlockSpec((pd, tn), lambda i, j: (0, j)),
        ],
        out_specs=pl.BlockSpec((tm, tn), lambda i, j: (i, j)),
        out_shape=jax.ShapeDtypeStruct((m, d), F32),
        scratch_shapes=[pltpu.VMEM((tm, d), BF16)],
        compiler_params=_params(2),
        name="ple",
    )(x, g, w_gate, x, p, w_proj)


def _matmul_residual_kernel(*refs, scale, has_bias):
    if has_bias:
        a_ref, w_ref, x_ref, b_ref, o_ref = refs
    else:
        a_ref, w_ref, x_ref, o_ref = refs
    k = pl.program_id(2)
    part = _dot(a_ref[...], w_ref[...])

    @pl.when(k == 0)
    def _():
        o_ref[...] = part

    @pl.when(k > 0)
    def _():
        o_ref[...] += part

    @pl.when(k == pl.num_programs(2) - 1)
    def _():
        y = o_ref[...]
        if has_bias:
            y = y + b_ref[...]
        o_ref[...] = x_ref[...] + scale * y


def matmul_residual(a, w, x, b=None, *, scale, tm, tn, tk):
    m, kdim = a.shape
    n = w.shape[1]
    in_specs = [
        pl.BlockSpec((tm, tk), lambda i, j, k: (i, k)),
        pl.BlockSpec((tk, tn), lambda i, j, k: (k, j)),
        pl.BlockSpec((tm, tn), lambda i, j, k: (i, j)),
    ]
    args = [a, w, x]
    if b is not None:
        in_specs.append(pl.BlockSpec((1, tn), lambda i, j, k: (0, j)))
        args.append(b)
    return pl.pallas_call(
        functools.partial(_matmul_residual_kernel, scale=scale, has_bias=b is not None),
        grid=(m // tm, n // tn, kdim // tk),
        in_specs=in_specs,
        out_specs=pl.BlockSpec((tm, tn), lambda i, j, k: (i, j)),
        out_shape=jax.ShapeDtypeStruct((m, n), F32),
        compiler_params=_params(3),
        name="matmul_residual",
    )(*args)


def _rmsnorm_kernel(x_ref, g_ref, o_ref):
    x = x_ref[...]
    o_ref[...] = x * lax.rsqrt(jnp.mean(x * x, axis=-1, keepdims=True) + NORM_EPS) * g_ref[...]


def rmsnorm(x, g, *, tm):
    m, d = x.shape
    return pl.pallas_call(
        _rmsnorm_kernel,
        grid=(m // tm,),
        in_specs=[pl.BlockSpec((tm, d), lambda i: (i, 0)), pl.BlockSpec((1, d), lambda i: (0, 0))],
        out_specs=pl.BlockSpec((tm, d), lambda i: (i, 0)),
        out_shape=jax.ShapeDtypeStruct((m, d), F32),
        compiler_params=_params(1),
        name="rmsnorm",
    )(x, g)


def _attn_prompt_kernel(sink_ref, q_ref, kp_ref, kc_ref, vp_ref, vc_ref, o_ref, *, n_kv, group, scale):
    blk = pl.program_id(1)
    t = WINDOW
    pairs = group // 2
    q = q_ref[...].astype(BF16)
    k_all = jnp.concatenate([kp_ref[...], kc_ref[...]], axis=0)
    v_all = jnp.concatenate([vp_ref[...], vc_ref[...]], axis=0)

    r = lax.broadcasted_iota(jnp.int32, (t, 2 * t), 0)
    c = lax.broadcasted_iota(jnp.int32, (t, 2 * t), 1)
    first_col = jnp.where(blk > 0, r, t)
    valid = (c >= first_col) & (c <= r + t)
    lane = lax.broadcasted_iota(jnp.int32, (2 * t, 2 * HEAD_DIM), 1)
    low = lane < HEAD_DIM

    for n in range(n_kv):
        slab = slice((n // 2) * 2 * HEAD_DIM, (n // 2 + 1) * 2 * HEAD_DIM)
        k2 = k_all[:, slab]
        v2 = v_all[:, slab]
        if n % 2 == 0:
            k_lo = jnp.where(low, k2, 0.0)
            v_lo = jnp.where(low, v2, 0.0)
            k_hi = pltpu.roll(k_lo, HEAD_DIM, 1)
            v_hi = pltpu.roll(v_lo, HEAD_DIM, 1)
        else:
            k_hi = jnp.where(low, 0.0, k2)
            v_hi = jnp.where(low, 0.0, v2)
            k_lo = pltpu.roll(k_hi, HEAD_DIM, 1)
            v_lo = pltpu.roll(v_hi, HEAD_DIM, 1)
        k_halves = (k_lo.astype(BF16), k_hi.astype(BF16))
        v_halves = (v_lo.astype(BF16), v_hi.astype(BF16))

        q_slabs = [q[:, (n * pairs + j) * 2 * HEAD_DIM:(n * pairs + j + 1) * 2 * HEAD_DIM] for j in range(pairs)]
        q4 = jnp.concatenate(q_slabs, axis=0)
        out = None
        for half in range(2):
            s = lax.dot_general(q4, k_halves[half], (((1,), (1,)), ((), ())),
                                preferred_element_type=F32) * scale
            probs = []
            for j in range(pairs):
                sink = sink_ref[n * group + 2 * j + half]
                sj = jnp.where(valid, s[j * t:(j + 1) * t], NEG_INF)
                m = jnp.maximum(jnp.max(sj, axis=-1, keepdims=True), sink)
                pj = jnp.exp(sj - m)
                pj = pj / (jnp.sum(pj, axis=-1, keepdims=True) + jnp.exp(sink - m))
                probs.append(pj.astype(BF16))
            part = _dot(jnp.concatenate(probs, axis=0), v_halves[half])
            out = part if out is None else out + part
        for j in range(pairs):
            col = (n * pairs + j) * 2 * HEAD_DIM
            o_ref[:, col:col + 2 * HEAD_DIM] = out[j * t:(j + 1) * t].astype(o_ref.dtype)


def attn_prompt(z, sinks, *, batch, seq, n_heads, n_kv):
    t = WINDOW
    nb = seq // t
    hq = n_heads * HEAD_DIM
    hk = n_kv * HEAD_DIM
    kcol = hq // hk
    vcol = kcol + 1

    def cur(b, i, s):
        return b * nb + i

    def prev(b, i, s):
        return b * nb + jnp.maximum(i - 1, 0)

    grid_spec = pltpu.PrefetchScalarGridSpec(
        num_scalar_prefetch=1,
        grid=(batch, nb),
        in_specs=[
            pl.BlockSpec((t, hq), lambda b, i, s: (cur(b, i, s), 0)),
            pl.BlockSpec((t, hk), lambda b, i, s: (prev(b, i, s), kcol)),
            pl.BlockSpec((t, hk), lambda b, i, s: (cur(b, i, s), kcol)),
            pl.BlockSpec((t, hk), lambda b, i, s: (prev(b, i, s), vcol)),
            pl.BlockSpec((t, hk), lambda b, i, s: (cur(b, i, s), vcol)),
        ],
        out_specs=pl.BlockSpec((t, hq), lambda b, i, s: (cur(b, i, s), 0)),
    )
    return pl.pallas_call(
        functools.partial(_attn_prompt_kernel, n_kv=n_kv, group=n_heads // n_kv, scale=HEAD_DIM ** -0.5),
        grid_spec=grid_spec,
        out_shape=jax.ShapeDtypeStruct((batch * seq, hq), BF16),
        compiler_params=_params(2),
        name="attn_prompt",
    )(sinks, z, z, z, z, z)


def _attn_sample_kernel(q_ref, kc_ref, vc_ref, kn_ref, vn_ref, sink_ref, o_ref, *, n_kv, group, scale, first_valid):
    q = q_ref[...]
    for n in range(n_kv):
        lanes = slice(n * HEAD_DIM, (n + 1) * HEAD_DIM)
        rows = slice(n * group, (n + 1) * group)
        qn = q[rows]
        kc = kc_ref[:, lanes]
        vc = vc_ref[:, lanes]
        kn = kn_ref[:, lanes]
        vn = vn_ref[:, lanes]
        sink = sink_ref[rows]
        s_c = lax.dot_general(qn.astype(BF16), kc.astype(BF16), (((1,), (1,)), ((), ())),
                              preferred_element_type=F32) * scale
        qb = qn.astype(BF16).astype(F32)
        kb = kn.astype(BF16).astype(F32)
        s_n = jnp.sum(qb * kb, axis=-1, keepdims=True) * scale
        if first_valid > 0:
            col = lax.broadcasted_iota(jnp.int32, s_c.shape, 1)
            s_c = jnp.where(col >= first_valid, s_c, NEG_INF)
        m = jnp.maximum(jnp.maximum(jnp.max(s_c, axis=-1, keepdims=True), s_n), sink)
        p_c = jnp.exp(s_c - m)
        p_n = jnp.exp(s_n - m)
        denom = jnp.sum(p_c, axis=-1, keepdims=True) + p_n + jnp.exp(sink - m)
        p_c = p_c / denom
        p_n = p_n / denom
        o = _dot(p_c.astype(BF16), vc.astype(BF16))
        o = o + p_n.astype(BF16).astype(F32) * vn.astype(BF16).astype(F32)
        o_ref[rows, :] = o


def attn_sample(q, kc, vc, kn, vn, sinks, *, n_kv):
    nseq, n_heads, hd = q.shape
    wb = kc.shape[1]
    hk = kc.shape[2]
    first_valid = max(0, wb - WINDOW, wb - PAST_LEN)
    return pl.pallas_call(
        functools.partial(_attn_sample_kernel, n_kv=n_kv, group=n_heads // n_kv, scale=HEAD_DIM ** -0.5,
                          first_valid=first_valid),
        grid=(nseq,),
        in_specs=[
            pl.BlockSpec((None, n_heads, hd), lambda b: (b, 0, 0)),
            pl.BlockSpec((None, wb, hk), lambda b: (b, 0, 0)),
            pl.BlockSpec((None, wb, hk), lambda b: (b, 0, 0)),
            pl.BlockSpec((None, 1, hk), lambda b: (b, 0, 0)),
            pl.BlockSpec((None, 1, hk), lambda b: (b, 0, 0)),
            pl.BlockSpec((n_heads, 1), lambda b: (0, 0)),
        ],
        out_specs=pl.BlockSpec((None, n_heads, hd), lambda b: (b, 0, 0)),
        out_shape=jax.ShapeDtypeStruct((nseq, n_heads, hd), F32),
        compiler_params=_params(1),
        name="attn_sample",
    )(q, kc, vc, kn, vn, sinks)


def _conv_mix_prompt_kernel(gb_ref, cu_ref, halo_ref, w_ref, o_ref, pad_ref, *, tiles_per_seq):
    tm = cu_ref.shape[0]
    first = (pl.program_id(0) % tiles_per_seq) == 0
    halo = jnp.where(first, 0.0, halo_ref[...])
    pad_ref[0:8, :] = halo
    pad_ref[8:8 + tm, :] = cu_ref[...]
    w = w_ref[...]
    dw = w[0:1] * pad_ref[6:6 + tm, :] + w[1:2] * pad_ref[7:7 + tm, :] + w[2:3] * pad_ref[8:8 + tm, :]
    o_ref[...] = (gb_ref[...] * dw).astype(o_ref.dtype)


def conv_mix_prompt(gb, cu, w, *, seq, tm, tn):
    m, d = gb.shape
    hb = tm // 8
    return pl.pallas_call(
        functools.partial(_conv_mix_prompt_kernel, tiles_per_seq=seq // tm),
        grid=(m // tm, d // tn),
        in_specs=[
            pl.BlockSpec((tm, tn), lambda i, j: (i, j)),
            pl.BlockSpec((tm, tn), lambda i, j: (i, j)),
            pl.BlockSpec((8, tn), lambda i, j: (jnp.maximum(i * hb - 1, 0), j)),
            pl.BlockSpec((CONV_WIDTH, tn), lambda i, j: (0, j)),
        ],
        out_specs=pl.BlockSpec((tm, tn), lambda i, j: (i, j)),
        out_shape=jax.ShapeDtypeStruct((m, d), BF16),
        scratch_shapes=[pltpu.VMEM((tm + 8, tn), F32)],
        compiler_params=_params(2),
        name="conv_mix_prompt",
    )(gb, cu, cu, w)


def _conv_mix_sample_kernel(gb_ref, cu_ref, s0_ref, s1_ref, w_ref, o_ref):
    w = w_ref[...]
    dw = w[0:1] * s0_ref[...] + w[1:2] * s1_ref[...] + w[2:3] * cu_ref[...]
    o_ref[...] = (gb_ref[...] * dw).astype(o_ref.dtype)


def conv_mix_sample(gb, cu, s0, s1, w):
    return pl.pallas_call(
        _conv_mix_sample_kernel,
        out_shape=jax.ShapeDtypeStruct(gb.shape, BF16),
        name="conv_mix_sample",
    )(gb, cu, s0, s1, w)


def kernel(x_prompt, x_sample, cache_k, cache_v, state_conv, p_prompt, p_sample, norm_gains, final_norm_gain,
           w_ffn_gate, w_ffn_up, w_ffn_down, w_qkv, b_qkv, attn_sinks, w_o, b_o, w_conv_in, conv_w, w_conv_out,
           w_ple_proj, w_ple_gate):
    batch, seq, d = x_prompt.shape
    nseq, dec_seq, _ = x_sample.shape
    assert dec_seq == 1 and seq % WINDOW == 0 and CONV_WIDTH == conv_w.shape[1]
    depth = norm_gains.shape[0]
    d_ff = w_ffn_gate.shape[-1]
    n_heads = attn_sinks.shape[1]
    qkv_dim = w_qkv.shape[-1]
    n_kv = (qkv_dim // HEAD_DIM - n_heads) // 2
    hq, hk = n_heads * HEAD_DIM, n_kv * HEAD_DIM
    wb = cache_k.shape[2]
    mp, ms = batch * seq, nseq

    f_pad = -(-d_ff // FFN_PAD) * FFN_PAD
    wg = jnp.pad(w_ffn_gate.astype(BF16), ((0, 0), (0, 0), (0, 0), (0, f_pad - d_ff)))
    wu = jnp.pad(w_ffn_up.astype(BF16), ((0, 0), (0, 0), (0, 0), (0, f_pad - d_ff)))
    wd = jnp.pad(w_ffn_down.astype(BF16), ((0, 0), (0, 0), (0, f_pad - d_ff), (0, 0)))
    wqkv = w_qkv.astype(BF16)
    wo = w_o.astype(BF16)
    wci = w_conv_in.astype(BF16)
    wco = w_conv_out.astype(BF16)
    wpp = w_ple_proj.astype(BF16)
    wpg = w_ple_gate.astype(BF16)

    xp = x_prompt.reshape(mp, d)
    xs = x_sample.reshape(ms, d)
    pp = p_prompt.reshape(depth, mp, -1)
    ps = p_sample.reshape(depth, ms, -1)

    tm_p = _tile(mp, 512)
    tm_s = ms

    def half_ffn(x, g, i, h, tm):
        act = swiglu_up(x, g, wg[i, h], wu[i, h], tm=tm, tn=_tile(f_pad, 512))
        return matmul_residual(act, wd[i, h], x, scale=0.5, tm=_tile(x.shape[0], 1024),
                               tn=_tile(d, 1024), tk=_tile(f_pad, 2816))

    nkp, nvp, nks, nvs, ncp, ncs = [], [], [], [], [], []
    for i in range(depth):
        g = norm_gains[i][:, None, :]
        xp = half_ffn(xp, g[0], i, 0, tm_p)
        xs = half_ffn(xs, g[0], i, 0, tm_s)
        j = i // 2
        if i % 2 == 0:
            bq = b_qkv[j][None, :]
            zp = norm_linear(xp, g[1], wqkv[j], bq, tm=tm_p, tn=_tile(qkv_dim, 512))
            zs = norm_linear(xs, g[1], wqkv[j], bq, tm=tm_s, tn=_tile(qkv_dim, 512))
            op = attn_prompt(zp, attn_sinks[j], batch=batch, seq=seq, n_heads=n_heads, n_kv=n_kv)
            kn = zs[:, hq:hq + hk]
            vn = zs[:, hq + hk:]
            os_ = attn_sample(zs[:, :hq].reshape(ms, n_heads, HEAD_DIM),
                              cache_k[j].reshape(ms, wb, hk), cache_v[j].reshape(ms, wb, hk),
                              kn[:, None, :], vn[:, None, :], attn_sinks[j][:, None],
                              n_kv=n_kv)
            os_ = os_.reshape(ms, hq).astype(BF16)
            bo = b_o[j][None, :]
            xp = matmul_residual(op, wo[j], xp, bo, scale=1.0, tm=_tile(mp, 1024), tn=_tile(d, 1024), tk=hq)
            xs = matmul_residual(os_, wo[j], xs, bo, scale=1.0, tm=tm_s, tn=_tile(d, 1024), tk=hq)
            zp3 = zp.reshape(batch, seq, qkv_dim)
            wp_ = min(WINDOW, seq)
            nkp.append(zp3[:, seq - wp_:, hq:hq + hk].reshape(batch, wp_, n_kv, HEAD_DIM))
            nvp.append(zp3[:, seq - wp_:, hq + hk:].reshape(batch, wp_, n_kv, HEAD_DIM))
            kk = jnp.concatenate([cache_k[j], kn.reshape(ms, 1, n_kv, HEAD_DIM)], axis=1)
            vv = jnp.concatenate([cache_v[j], vn.reshape(ms, 1, n_kv, HEAD_DIM)], axis=1)
            nks.append(kk[:, -wb:])
            nvs.append(vv[:, -wb:])
        else:
            gbp, cup = conv_gates(xp, g[1], wci[j], tm=tm_p, tn=_tile(d, 256))
            gbs, cus = conv_gates(xs, g[1], wci[j], tm=tm_s, tn=_tile(d, 256))
            tp = conv_mix_prompt(gbp, cup, conv_w[j], seq=seq, tm=_tile(seq, 512), tn=_tile(d, 1024))
            st = state_conv[j]
            ts = conv_mix_sample(gbs, cus, st[:, 0], st[:, 1], conv_w[j])
            xp = matmul_residual(tp, wco[j], xp, scale=1.0, tm=_tile(mp, 1024), tn=_tile(d, 1024), tk=d)
            xs = matmul_residual(ts, wco[j], xs, scale=1.0, tm=tm_s, tn=_tile(d, 1024), tk=d)
            ncp.append(cup.reshape(batch, seq, d)[:, seq - (CONV_WIDTH - 1):])
            ncs.append(jnp.concatenate([st, cus[:, None, :]], axis=1)[:, -(CONV_WIDTH - 1):])
        xp = half_ffn(xp, g[2], i, 1, tm_p)
        xs = half_ffn(xs, g[2], i, 1, tm_s)
        xp = ple(xp, g[3], wpg[i], pp[i], wpp[i], tm=tm_p, tn=_tile(d, 512))
        xs = ple(xs, g[3], wpg[i], ps[i], wpp[i], tm=tm_s, tn=_tile(d, 512))

    gf = final_norm_gain[None, :]
    y_prompt = rmsnorm(xp, gf, tm=_tile(mp, 256)).reshape(batch, seq, d)
    y_sample = rmsnorm(xs, gf, tm=ms).reshape(nseq, dec_seq, d)
    return (y_prompt, y_sample, jnp.stack(nkp), jnp.stack(nvp), jnp.stack(nks), jnp.stack(nvs),
            jnp.stack(ncp), jnp.stack(ncs))
```

```python
import functools
from typing import NamedTuple

import jax
import jax.numpy as jnp
from jax import lax
from jax.experimental import pallas as pl
from jax.experimental.pallas import tpu as pltpu

NORM_EPS = 1e-6
NEG_INF = -1e30
HEAD_DIM = 64
WINDOW = 128
CONV_WIDTH = 3
PAST_LEN = 8192

V7X_VMEM_LIMIT_BYTES = 58 * 1024 * 1024
BF16_SUBLANES = 16

ROW_TILE = 512
ROW_SUB = 2
PANEL_FFN = 512
PANEL_QKV = 512
PANEL_CONV = 256
PANEL_PLE = 512
PANEL_OUT = 1024
DOWN_ROWS = 1024
DOWN_K = 2816

BF16 = jnp.bfloat16
F32 = jnp.float32


def _params(n_axes):
    return pltpu.CompilerParams(
        dimension_semantics=("arbitrary",) * n_axes,
        vmem_limit_bytes=V7X_VMEM_LIMIT_BYTES,
    )


def _tile(dim, target):
    if dim <= target:
        return dim
    t = target
    while dim % t:
        t //= 2
    return t


def _cdiv(a, b):
    return -(-a // b)


def _dot(a, b):
    return jnp.dot(a, b, preferred_element_type=F32)


class CastJob(NamedTuple):
    src: jax.Array
    lead: tuple
    tn: int
    rows_out: int


class _JobPlan(NamedTuple):
    rows: int
    cols: int
    tn: int
    panels: int
    rb: int
    nb_src: int
    nb_dst: int
    rows_out: int


def _plan_job(job, n_steps):
    rows, cols = job.src.shape[-2:]
    rb = BF16_SUBLANES
    while _cdiv(job.rows_out, rb) > n_steps:
        rb *= 2
    assert rows % rb == 0 and job.rows_out % rb == 0, (rows, job.rows_out, rb)
    return _JobPlan(rows, cols, job.tn, _cdiv(cols, job.tn), rb, rows // rb, job.rows_out // rb, job.rows_out)


def _job_specs(job, plan, step_of):
    n_lead = len(job.lead)

    def src_idx(*g):
        return (*job.lead, jnp.minimum(step_of(*g), plan.nb_src - 1), 0)

    def dst_idx(*g):
        return (0, jnp.minimum(step_of(*g), plan.nb_dst - 1), 0)

    src_spec = pl.BlockSpec((None,) * n_lead + (plan.rb, plan.cols), src_idx)
    dst_spec = pl.BlockSpec((plan.panels, plan.rb, plan.tn), dst_idx)
    dst_shape = jax.ShapeDtypeStruct((plan.panels, plan.rows_out, plan.tn), BF16)
    return src_spec, dst_spec, dst_shape


def _run_job(plan, src_ref, dst_ref, step):
    v = src_ref[...]
    if plan.rows_out > plan.rows:
        blk = jnp.minimum(step, plan.nb_dst - 1)
        row = blk * plan.rb + lax.broadcasted_iota(jnp.int32, (plan.rb, 1), 0)
        v = jnp.where(row < plan.rows, v, 0.0)
    for p in range(plan.panels):
        lo = p * plan.tn
        width = min(plan.tn, plan.cols - lo)
        dst_ref[p, :, 0:width] = v[:, lo:lo + width].astype(BF16)
        if width < plan.tn:
            dst_ref[p, :, width:plan.tn] = jnp.zeros((plan.rb, plan.tn - width), BF16)


def _call_with_jobs(body, *, name, grid, in_specs, out_specs, out_shape, scratch_shapes, args, jobs):
    n_steps = 1
    for gdim in grid:
        n_steps *= gdim

    def step_of(*g):
        s = g[0]
        for gdim, gi in zip(grid[1:], g[1:]):
            s = s * gdim + gi
        return s

    plans = [_plan_job(j, n_steps) for j in jobs]
    specs = [_job_specs(j, p, step_of) for j, p in zip(jobs, plans)]
    n_in, n_out, n_job = len(in_specs), len(out_specs), len(jobs)

    def kern(*refs):
        ins = refs[:n_in]
        job_src = refs[n_in:n_in + n_job]
        outs = refs[n_in + n_job:n_in + n_job + n_out]
        job_dst = refs[n_in + n_job + n_out:n_in + 2 * n_job + n_out]
        scratch = refs[n_in + 2 * n_job + n_out:]
        body(*ins, *outs, *scratch)
        if n_job:
            step = step_of(*[pl.program_id(a) for a in range(len(grid))])
            for plan, s_ref, d_ref in zip(plans, job_src, job_dst):
                _run_job(plan, s_ref, d_ref, step)

    res = pl.pallas_call(
        kern,
        grid=grid,
        in_specs=list(in_specs) + [s[0] for s in specs],
        out_specs=list(out_specs) + [s[1] for s in specs],
        out_shape=list(out_shape) + [s[2] for s in specs],
        scratch_shapes=scratch_shapes,
        compiler_params=_params(len(grid)),
        name=name,
    )(*args, *[j.src for j in jobs])
    return list(res[:n_out]), list(res[n_out:])


def cast_panels(job):
    rows = job.rows_out
    n_steps = _cdiv(rows, 2 * BF16_SUBLANES)
    _, copies = _call_with_jobs(lambda: None, name="cast_panels", grid=(n_steps,), in_specs=[], out_specs=[],
                                out_shape=[], scratch_shapes=[], args=[], jobs=[job])
    return copies[0]


def _norm_rows_into(h_ref, row0, x_ref, g_ref):
    g = g_ref[...]
    chunk = BF16_SUBLANES

    def body(c, carry):
        r = pl.multiple_of(c * chunk, chunk)
        x = x_ref[pl.ds(r, chunk), :]
        y = x * lax.rsqrt(jnp.mean(x * x, axis=-1, keepdims=True) + NORM_EPS) * g
        h_ref[pl.ds(row0 + r, chunk), :] = y.astype(BF16)
        return carry

    n_chunks = x_ref.shape[0] // chunk
    lax.fori_loop(0, n_chunks, body, 0, unroll=min(n_chunks, 8))


def _norm_matmul(name, epilogue, x, g, panels, extras, outs, *, tm, sub, n_panels, jobs=()):
    m, d = x.shape
    assert m % (tm * sub) == 0
    grid = (m // (tm * sub), n_panels, sub)
    n_p, n_e, n_o = len(panels), len(extras), len(outs)

    def x_idx(i, n, s):
        return (jnp.where(n == 0, i * sub + s, i * sub + sub - 1), 0)

    in_specs = [pl.BlockSpec((tm, d), x_idx), pl.BlockSpec((1, d), lambda i, n, s: (0, 0))]
    for w, off in panels:
        in_specs.append(pl.BlockSpec((None, d, w.shape[2]), functools.partial(lambda i, n, s, o: (n + o, 0, 0), o=off)))
    for _, shape, idx in extras:
        in_specs.append(pl.BlockSpec(shape, functools.partial(lambda i, n, s, f: f(i * sub + s, n), f=idx)))
    out_specs = [pl.BlockSpec(shape, functools.partial(lambda i, n, s, f: f(i * sub + s, n), f=idx))
                 for _, shape, idx in outs]

    def body(x_ref, g_ref, *rest):
        panel_refs = rest[:n_p]
        extra_refs = rest[n_p:n_p + n_e]
        out_refs = rest[n_p + n_e:n_p + n_e + n_o]
        h_ref = rest[n_p + n_e + n_o]
        row0 = pl.multiple_of(pl.program_id(2) * tm, tm)

        @pl.when(pl.program_id(1) == 0)
        def _():
            _norm_rows_into(h_ref, row0, x_ref, g_ref)

        epilogue(h_ref[pl.ds(row0, tm), :], panel_refs, extra_refs, out_refs)

    return _call_with_jobs(
        body, name=name, grid=grid, in_specs=in_specs, out_specs=out_specs,
        out_shape=[o[0] for o in outs], scratch_shapes=[pltpu.VMEM((tm * sub, d), BF16)],
        args=[x, g] + [w for w, _ in panels] + [e[0] for e in extras], jobs=list(jobs))


def _swiglu_epilogue(h, panel_refs, extra_refs, out_refs):
    gate = _dot(h, panel_refs[0][...])
    up = _dot(h, panel_refs[1][...])
    out_refs[0][...] = (gate * jax.nn.sigmoid(gate) * up).astype(BF16)


def swiglu_up(x, g, wg, wu, *, tm, sub, jobs=()):
    m = x.shape[0]
    n_panels, _, tn = wg.shape
    outs, copies = _norm_matmul(
        "swiglu_up", _swiglu_epilogue, x, g, [(wg, 0), (wu, 0)], [],
        [(jax.ShapeDtypeStruct((m, n_panels * tn), BF16), (tm, tn), lambda r, n: (r, n))],
        tm=tm, sub=sub, n_panels=n_panels, jobs=jobs)
    return outs[0], copies


def _linear_epilogue(h, panel_refs, extra_refs, out_refs):
    out_refs[0][...] = _dot(h, panel_refs[0][...]) + extra_refs[0][...]


def norm_linear(x, g, w, b, *, tm, sub, jobs=()):
    m = x.shape[0]
    n_panels, _, tn = w.shape
    outs, copies = _norm_matmul(
        "norm_linear", _linear_epilogue, x, g, [(w, 0)], [(b, (1, tn), lambda r, n: (0, n))],
        [(jax.ShapeDtypeStruct((m, n_panels * tn), F32), (tm, tn), lambda r, n: (r, n))],
        tm=tm, sub=sub, n_panels=n_panels, jobs=jobs)
    return outs[0], copies


def _conv_gates_epilogue(h, panel_refs, extra_refs, out_refs):
    out_refs[0][...] = _dot(h, panel_refs[0][...])
    out_refs[1][...] = _dot(h, panel_refs[1][...]) * _dot(h, panel_refs[2][...])


def conv_gates(x, g, w_in, *, tm, sub, jobs=()):
    m, d = x.shape
    tn = w_in.shape[2]
    n_panels = d // tn
    out = (jax.ShapeDtypeStruct((m, d), F32), (tm, tn), lambda r, n: (r, n))
    outs, copies = _norm_matmul(
        "conv_gates", _conv_gates_epilogue, x, g, [(w_in, 0), (w_in, n_panels), (w_in, 2 * n_panels)], [],
        [out, out], tm=tm, sub=sub, n_panels=n_panels, jobs=jobs)
    return outs[0], outs[1], copies


def _ple_epilogue(h, panel_refs, extra_refs, out_refs):
    x_ref, p_ref, wp_ref = extra_refs
    gate = jax.nn.sigmoid(_dot(h, panel_refs[0][...]))
    proj = _dot(p_ref[...].astype(BF16), wp_ref[...])
    out_refs[0][...] = x_ref[...] + gate * proj


def ple(x, g, w_gate, p, w_proj, *, tm, sub, jobs=()):
    m, d = x.shape
    pd = p.shape[1]
    n_panels, _, tn = w_gate.shape
    extras = [
        (x, (tm, tn), lambda r, n: (r, n)),
        (p, (tm, pd), lambda r, n: (r, 0)),
        (w_proj, (pd, tn), lambda r, n: (0, n)),
    ]
    outs, copies = _norm_matmul(
        "ple", _ple_epilogue, x, g, [(w_gate, 0)], extras,
        [(jax.ShapeDtypeStruct((m, d), F32), (tm, tn), lambda r, n: (r, n))],
        tm=tm, sub=sub, n_panels=n_panels, jobs=jobs)
    return outs[0], copies


def _matmul_residual_body(*refs, scale, has_bias, n_k):
    a_ref, w_ref, x_ref = refs[:3]
    b_ref = refs[3] if has_bias else None
    o_ref = refs[3 + has_bias]
    tn = o_ref.shape[1]
    halves = [slice(c * (tn // 2), (c + 1) * (tn // 2)) for c in range(2)] if tn % 256 == 0 else [slice(0, tn)]

    def finish(y, cols):
        if has_bias:
            y = y + b_ref[:, cols]
        return x_ref[:, cols] + scale * y

    if n_k == 1:
        for cols in halves:
            o_ref[:, cols] = finish(_dot(a_ref[...], w_ref[:, cols]), cols)
        return

    acc_ref = refs[4 + has_bias]
    k = pl.program_id(2)

    @pl.when((pl.program_id(0) == 0) & (pl.program_id(1) == 0) & (k == 0))
    def _():
        acc_ref[...] = jnp.zeros_like(acc_ref)

    for cols in halves:
        acc = acc_ref[:, cols] + _dot(a_ref[...], w_ref[:, cols])
        acc_ref[:, cols] = jnp.where(k == n_k - 1, 0.0, acc)
        o_ref[:, cols] = finish(acc, cols)


def matmul_residual(a, w, x, b=None, *, scale, tm, tk, jobs=()):
    m, kdim = a.shape
    n_panels, kw, tn = w.shape
    assert kw == kdim and kdim % tk == 0 and m % tm == 0
    n_k = kdim // tk
    in_specs = [
        pl.BlockSpec((tm, tk), lambda i, j, k: (i, k)),
        pl.BlockSpec((None, tk, tn), lambda i, j, k: (j, k, 0)),
        pl.BlockSpec((tm, tn), lambda i, j, k: (i, j)),
    ]
    args = [a, w, x]
    if b is not None:
        in_specs.append(pl.BlockSpec((1, tn), lambda i, j, k: (0, j)))
        args.append(b)
    outs, copies = _call_with_jobs(
        functools.partial(_matmul_residual_body, scale=scale, has_bias=b is not None, n_k=n_k),
        name="matmul_residual", grid=(m // tm, n_panels, n_k), in_specs=in_specs,
        out_specs=[pl.BlockSpec((tm, tn), lambda i, j, k: (i, j))],
        out_shape=[jax.ShapeDtypeStruct((m, n_panels * tn), F32)],
        scratch_shapes=[pltpu.VMEM((tm, tn), F32)] if n_k > 1 else [], args=args, jobs=list(jobs))
    return outs[0], copies


def _rmsnorm_kernel(x_ref, g_ref, o_ref):
    x = x_ref[...]
    o_ref[...] = x * lax.rsqrt(jnp.mean(x * x, axis=-1, keepdims=True) + NORM_EPS) * g_ref[...]


def rmsnorm(x, g, *, tm):
    m, d = x.shape
    return pl.pallas_call(
        _rmsnorm_kernel,
        grid=(m // tm,),
        in_specs=[pl.BlockSpec((tm, d), lambda i: (i, 0)), pl.BlockSpec((1, d), lambda i: (0, 0))],
        out_specs=pl.BlockSpec((tm, d), lambda i: (i, 0)),
        out_shape=jax.ShapeDtypeStruct((m, d), F32),
        compiler_params=_params(1),
        name="rmsnorm",
    )(x, g)


def _attn_prompt_body(sink_ref, q_ref, kp_ref, kc_ref, vp_ref, vc_ref, o_ref, *, n_kv, group, scale):
    blk = pl.program_id(1)
    t = WINDOW
    pairs = group // 2
    q = q_ref[...].astype(BF16)
    k_all = jnp.concatenate([kp_ref[...], kc_ref[...]], axis=0)
    v_all = jnp.concatenate([vp_ref[...], vc_ref[...]], axis=0)

    r = lax.broadcasted_iota(jnp.int32, (t, 2 * t), 0)
    c = lax.broadcasted_iota(jnp.int32, (t, 2 * t), 1)
    first_col = jnp.where(blk > 0, r, t)
    valid = (c >= first_col) & (c <= r + t)
    lane = lax.broadcasted_iota(jnp.int32, (2 * t, 2 * HEAD_DIM), 1)
    low = lane < HEAD_DIM

    for n in range(n_kv):
        slab = slice((n // 2) * 2 * HEAD_DIM, (n // 2 + 1) * 2 * HEAD_DIM)
        k2 = k_all[:, slab]
        v2 = v_all[:, slab]
        if n % 2 == 0:
            k_lo = jnp.where(low, k2, 0.0)
            v_lo = jnp.where(low, v2, 0.0)
            k_hi = pltpu.roll(k_lo, HEAD_DIM, 1)
            v_hi = pltpu.roll(v_lo, HEAD_DIM, 1)
        else:
            k_hi = jnp.where(low, 0.0, k2)
            v_hi = jnp.where(low, 0.0, v2)
            k_lo = pltpu.roll(k_hi, HEAD_DIM, 1)
            v_lo = pltpu.roll(v_hi, HEAD_DIM, 1)
        k_halves = (k_lo.astype(BF16), k_hi.astype(BF16))
        v_halves = (v_lo.astype(BF16), v_hi.astype(BF16))

        q_slabs = [q[:, (n * pairs + j) * 2 * HEAD_DIM:(n * pairs + j + 1) * 2 * HEAD_DIM] for j in range(pairs)]
        q4 = jnp.concatenate(q_slabs, axis=0)
        out = None
        for half in range(2):
            s = lax.dot_general(q4, k_halves[half], (((1,), (1,)), ((), ())),
                                preferred_element_type=F32) * scale
            probs = []
            for j in range(pairs):
                sink = sink_ref[n * group + 2 * j + half]
                sj = jnp.where(valid, s[j * t:(j + 1) * t], NEG_INF)
                m = jnp.maximum(jnp.max(sj, axis=-1, keepdims=True), sink)
                pj = jnp.exp(sj - m)
                pj = pj / (jnp.sum(pj, axis=-1, keepdims=True) + jnp.exp(sink - m))
                probs.append(pj.astype(BF16))
            part = _dot(jnp.concatenate(probs, axis=0), v_halves[half])
            out = part if out is None else out + part
        for j in range(pairs):
            col = (n * pairs + j) * 2 * HEAD_DIM
            o_ref[:, col:col + 2 * HEAD_DIM] = out[j * t:(j + 1) * t].astype(o_ref.dtype)


def attn_prompt(z, sinks, *, batch, seq, n_heads, n_kv):
    t = WINDOW
    nb = seq // t
    hq = n_heads * HEAD_DIM
    hk = n_kv * HEAD_DIM
    kcol = hq // hk
    vcol = kcol + 1

    def cur(b, i, s):
        return b * nb + i

    def prev(b, i, s):
        return b * nb + jnp.maximum(i - 1, 0)

    grid_spec = pltpu.PrefetchScalarGridSpec(
        num_scalar_prefetch=1,
        grid=(batch, nb),
        in_specs=[
            pl.BlockSpec((t, hq), lambda b, i, s: (cur(b, i, s), 0)),
            pl.BlockSpec((t, hk), lambda b, i, s: (prev(b, i, s), kcol)),
            pl.BlockSpec((t, hk), lambda b, i, s: (cur(b, i, s), kcol)),
            pl.BlockSpec((t, hk), lambda b, i, s: (prev(b, i, s), vcol)),
            pl.BlockSpec((t, hk), lambda b, i, s: (cur(b, i, s), vcol)),
        ],
        out_specs=pl.BlockSpec((t, hq), lambda b, i, s: (cur(b, i, s), 0)),
    )
    return pl.pallas_call(
        functools.partial(_attn_prompt_body, n_kv=n_kv, group=n_heads // n_kv, scale=HEAD_DIM ** -0.5),
        grid_spec=grid_spec,
        out_shape=jax.ShapeDtypeStruct((batch * seq, hq), BF16),
        compiler_params=_params(2),
        name="attn_prompt",
    )(sinks, z, z, z, z, z)


def _attn_sample_body(q_ref, kc_ref, vc_ref, kn_ref, vn_ref, sink_ref, o_ref, *, n_kv, group, scale, first_valid):
    q = q_ref[...]
    for n in range(n_kv):
        lanes = slice(n * HEAD_DIM, (n + 1) * HEAD_DIM)
        rows = slice(n * group, (n + 1) * group)
        qn = q[rows]
        kc = kc_ref[:, lanes]
        vc = vc_ref[:, lanes]
        kn = kn_ref[:, lanes]
        vn = vn_ref[:, lanes]
        sink = sink_ref[rows]
        s_c = lax.dot_general(qn.astype(BF16), kc.astype(BF16), (((1,), (1,)), ((), ())),
                              preferred_element_type=F32) * scale
        qb = qn.astype(BF16).astype(F32)
        kb = kn.astype(BF16).astype(F32)
        s_n = jnp.sum(qb * kb, axis=-1, keepdims=True) * scale
        if first_valid > 0:
            col = lax.broadcasted_iota(jnp.int32, s_c.shape, 1)
            s_c = jnp.where(col >= first_valid, s_c, NEG_INF)
        m = jnp.maximum(jnp.maximum(jnp.max(s_c, axis=-1, keepdims=True), s_n), sink)
        p_c = jnp.exp(s_c - m)
        p_n = jnp.exp(s_n - m)
        denom = jnp.sum(p_c, axis=-1, keepdims=True) + p_n + jnp.exp(sink - m)
        p_c = p_c / denom
        p_n = p_n / denom
        o = _dot(p_c.astype(BF16), vc.astype(BF16))
        o = o + p_n.astype(BF16).astype(F32) * vn.astype(BF16).astype(F32)
        o_ref[rows, :] = o


def attn_sample(q, kc, vc, kn, vn, sinks, *, n_kv):
    nseq, n_heads, hd = q.shape
    wb = kc.shape[1]
    hk = kc.shape[2]
    first_valid = max(0, wb - WINDOW, wb - PAST_LEN)
    return pl.pallas_call(
        functools.partial(_attn_sample_body, n_kv=n_kv, group=n_heads // n_kv, scale=HEAD_DIM ** -0.5,
                          first_valid=first_valid),
        grid=(nseq,),
        in_specs=[
            pl.BlockSpec((None, n_heads, hd), lambda b: (b, 0, 0)),
            pl.BlockSpec((None, wb, hk), lambda b: (b, 0, 0)),
            pl.BlockSpec((None, wb, hk), lambda b: (b, 0, 0)),
            pl.BlockSpec((None, 1, hk), lambda b: (b, 0, 0)),
            pl.BlockSpec((None, 1, hk), lambda b: (b, 0, 0)),
            pl.BlockSpec((n_heads, 1), lambda b: (0, 0)),
        ],
        out_specs=pl.BlockSpec((None, n_heads, hd), lambda b: (b, 0, 0)),
        out_shape=jax.ShapeDtypeStruct((nseq, n_heads, hd), F32),
        compiler_params=_params(1),
        name="attn_sample",
    )(q, kc, vc, kn, vn, sinks)


def _conv_mix_prompt_body(gb_ref, cu_ref, halo_ref, w_ref, o_ref, pad_ref, *, tiles_per_seq):
    tm = cu_ref.shape[0]
    first = (pl.program_id(0) % tiles_per_seq) == 0
    halo = jnp.where(first, 0.0, halo_ref[...])
    pad_ref[0:8, :] = halo
    pad_ref[8:8 + tm, :] = cu_ref[...]
    w = w_ref[...]
    dw = w[0:1] * pad_ref[6:6 + tm, :] + w[1:2] * pad_ref[7:7 + tm, :] + w[2:3] * pad_ref[8:8 + tm, :]
    o_ref[...] = (gb_ref[...] * dw).astype(o_ref.dtype)


def conv_mix_prompt(gb, cu, w, *, seq, tm, tn):
    m, d = gb.shape
    hb = tm // 8
    return pl.pallas_call(
        functools.partial(_conv_mix_prompt_body, tiles_per_seq=seq // tm),
        grid=(m // tm, d // tn),
        in_specs=[
            pl.BlockSpec((tm, tn), lambda i, j: (i, j)),
            pl.BlockSpec((tm, tn), lambda i, j: (i, j)),
            pl.BlockSpec((8, tn), lambda i, j: (jnp.maximum(i * hb - 1, 0), j)),
            pl.BlockSpec((CONV_WIDTH, tn), lambda i, j: (0, j)),
        ],
        out_specs=pl.BlockSpec((tm, tn), lambda i, j: (i, j)),
        out_shape=jax.ShapeDtypeStruct((m, d), BF16),
        scratch_shapes=[pltpu.VMEM((tm + 8, tn), F32)],
        compiler_params=_params(2),
        name="conv_mix_prompt",
    )(gb, cu, cu, w)


def _conv_mix_sample_body(gb_ref, cu_ref, s0_ref, s1_ref, w_ref, o_ref):
    w = w_ref[...]
    dw = w[0:1] * s0_ref[...] + w[1:2] * s1_ref[...] + w[2:3] * cu_ref[...]
    o_ref[...] = (gb_ref[...] * dw).astype(o_ref.dtype)


def conv_mix_sample(gb, cu, s0, s1, w):
    return pl.pallas_call(
        _conv_mix_sample_body,
        out_shape=jax.ShapeDtypeStruct(gb.shape, BF16),
        name="conv_mix_sample",
    )(gb, cu, s0, s1, w)


def kernel(x_prompt, x_sample, cache_k, cache_v, state_conv, p_prompt, p_sample, norm_gains, final_norm_gain,
           w_ffn_gate, w_ffn_up, w_ffn_down, w_qkv, b_qkv, attn_sinks, w_o, b_o, w_conv_in, conv_w, w_conv_out,
           w_ple_proj, w_ple_gate):
    batch, seq, d = x_prompt.shape
    nseq, dec_seq, _ = x_sample.shape
    assert dec_seq == 1 and seq % WINDOW == 0 and CONV_WIDTH == conv_w.shape[1]
    depth = norm_gains.shape[0]
    d_ff = w_ffn_gate.shape[-1]
    n_heads = attn_sinks.shape[1]
    qkv_dim = w_qkv.shape[-1]
    n_kv = (qkv_dim // HEAD_DIM - n_heads) // 2
    hq, hk = n_heads * HEAD_DIM, n_kv * HEAD_DIM
    wb = cache_k.shape[2]
    mp, ms = batch * seq, nseq

    f_pad = _cdiv(d_ff, PANEL_FFN) * PANEL_FFN
    tm_p = _tile(mp, ROW_TILE)
    sub_p = ROW_SUB if mp % (tm_p * ROW_SUB) == 0 else 1
    tm_d = _tile(mp, DOWN_ROWS)
    tk_d = _tile(f_pad, DOWN_K)
    panel_qkv = _tile(qkv_dim, PANEL_QKV)
    panel_conv = _tile(d, PANEL_CONV)
    panel_ple = _tile(d, PANEL_PLE)
    panel_out = _tile(d, PANEL_OUT)

    def job_gate(i, h):
        return CastJob(w_ffn_gate, (i, h), PANEL_FFN, d)

    def job_up(i, h):
        return CastJob(w_ffn_up, (i, h), PANEL_FFN, d)

    def job_down(i, h):
        return CastJob(w_ffn_down, (i, h), panel_out, f_pad)

    def job_mixer_in(i):
        if i % 2 == 0:
            return CastJob(w_qkv, (i // 2,), panel_qkv, d)
        return CastJob(w_conv_in, (i // 2,), panel_conv, d)

    def job_mixer_out(i):
        if i % 2 == 0:
            return CastJob(w_o, (i // 2,), panel_out, hq)
        return CastJob(w_conv_out, (i // 2,), panel_out, d)

    def job_ple_gate(i):
        return CastJob(w_ple_gate, (i,), panel_ple, d)

    wpp = w_ple_proj.astype(BF16)

    xp = x_prompt.reshape(mp, d)
    xs = x_sample.reshape(ms, d)
    pp = p_prompt.reshape(depth, mp, -1)
    ps = p_sample.reshape(depth, ms, -1)

    wg = cast_panels(job_gate(0, 0))
    wu = cast_panels(job_up(0, 0))

    nkp, nvp, nks, nvs, ncp, ncs = [], [], [], [], [], []
    for i in range(depth):
        g = norm_gains[i][:, None, :]
        last = i == depth - 1
        j = i // 2

        act, (wd, w_mi, wu_next) = swiglu_up(xp, g[0], wg, wu, tm=tm_p, sub=sub_p,
                                             jobs=[job_down(i, 0), job_mixer_in(i), job_up(i, 1)])
        act_s, _ = swiglu_up(xs, g[0], wg, wu, tm=ms, sub=1)
        xp, (w_mo,) = matmul_residual(act, wd, xp, scale=0.5, tm=tm_d, tk=tk_d, jobs=[job_mixer_out(i)])
        xs, _ = matmul_residual(act_s, wd, xs, scale=0.5, tm=ms, tk=tk_d)

        if i % 2 == 0:
            bq = b_qkv[j][None, :]
            zp, (wg_next,) = norm_linear(xp, g[1], w_mi, bq, tm=tm_p, sub=sub_p, jobs=[job_gate(i, 1)])
            zs, _ = norm_linear(xs, g[1], w_mi, bq, tm=ms, sub=1)
            op = attn_prompt(zp, attn_sinks[j], batch=batch, seq=seq, n_heads=n_heads, n_kv=n_kv)
            kn = zs[:, hq:hq + hk]
            vn = zs[:, hq + hk:]
            os_ = attn_sample(zs[:, :hq].reshape(ms, n_heads, HEAD_DIM),
                              cache_k[j].reshape(ms, wb, hk), cache_v[j].reshape(ms, wb, hk),
                              kn[:, None, :], vn[:, None, :], attn_sinks[j][:, None],
                              n_kv=n_kv)
            os_ = os_.reshape(ms, hq).astype(BF16)
            bo = b_o[j][None, :]
            xp, _ = matmul_residual(op, w_mo, xp, bo, scale=1.0, tm=tm_d, tk=hq)
            xs, _ = matmul_residual(os_, w_mo, xs, bo, scale=1.0, tm=ms, tk=hq)
            zp3 = zp.reshape(batch, seq, qkv_dim)
            wp_ = min(WINDOW, seq)
            nkp.append(zp3[:, seq - wp_:, hq:hq + hk].reshape(batch, wp_, n_kv, HEAD_DIM))
            nvp.append(zp3[:, seq - wp_:, hq + hk:].reshape(batch, wp_, n_kv, HEAD_DIM))
            kk = jnp.concatenate([cache_k[j], kn.reshape(ms, 1, n_kv, HEAD_DIM)], axis=1)
            vv = jnp.concatenate([cache_v[j], vn.reshape(ms, 1, n_kv, HEAD_DIM)], axis=1)
            nks.append(kk[:, -wb:])
            nvs.append(vv[:, -wb:])
        else:
            gbp, cup, (wg_next,) = conv_gates(xp, g[1], w_mi, tm=tm_p, sub=sub_p, jobs=[job_gate(i, 1)])
            gbs, cus, _ = conv_gates(xs, g[1], w_mi, tm=ms, sub=1)
            tp = conv_mix_prompt(gbp, cup, conv_w[j], seq=seq, tm=_tile(seq, 512), tn=_tile(d, 1024))
            st = state_conv[j]
            ts = conv_mix_sample(gbs, cus, st[:, 0], st[:, 1], conv_w[j])
            xp, _ = matmul_residual(tp, w_mo, xp, scale=1.0, tm=tm_d, tk=d)
            xs, _ = matmul_residual(ts, w_mo, xs, scale=1.0, tm=ms, tk=d)
            ncp.append(cup.reshape(batch, seq, d)[:, seq - (CONV_WIDTH - 1):])
            ncs.append(jnp.concatenate([st, cus[:, None, :]], axis=1)[:, -(CONV_WIDTH - 1):])

        wg, wu = wg_next, wu_next
        act, (wd, w_pg) = swiglu_up(xp, g[2], wg, wu, tm=tm_p, sub=sub_p, jobs=[job_down(i, 1), job_ple_gate(i)])
        act_s, _ = swiglu_up(xs, g[2], wg, wu, tm=ms, sub=1)
        xp, copies = matmul_residual(act, wd, xp, scale=0.5, tm=tm_d, tk=tk_d,
                                     jobs=[] if last else [job_gate(i + 1, 0)])
        xs, _ = matmul_residual(act_s, wd, xs, scale=0.5, tm=ms, tk=tk_d)
        if not last:
            wg = copies[0]

        xp, copies = ple(xp, g[3], w_pg, pp[i], wpp[i], tm=tm_p, sub=sub_p, jobs=[] if last else [job_up(i + 1, 0)])
        xs, _ = ple(xs, g[3], w_pg, ps[i], wpp[i], tm=ms, sub=1)
        if not last:
            wu = copies[0]

    gf = final_norm_gain[None, :]
    y_prompt = rmsnorm(xp, gf, tm=_tile(mp, 256)).reshape(batch, seq, d)
    y_sample = rmsnorm(xs, gf, tm=ms).reshape(nseq, dec_seq, d)
    return (y_prompt, y_sample, jnp.stack(nkp), jnp.stack(nvp), jnp.stack(nks), jnp.stack(nvs),
            jnp.stack(ncp), jnp.stack(ncs))
```

```python
import functools
from typing import NamedTuple

import jax
import jax.numpy as jnp
from jax import lax
from jax.experimental import pallas as pl
from jax.experimental.pallas import tpu as pltpu

NORM_EPS = 1e-6
NEG_INF = -1e30
HEAD_DIM = 64
WINDOW = 128
CONV_WIDTH = 3
PAST_LEN = 8192

V7X_VMEM_LIMIT_BYTES = 58 * 1024 * 1024
BF16_SUBLANES = 16

GROUP_ROWS = 1024
ROW_TILE = 512
NORM_ROWS = 256
PANEL_FFN = 512
PANEL_QKV = 512
PANEL_CONV = 256
PANEL_PLE = 512
PANEL_OUT = 1024
DOWN_ROWS = 1024
DOWN_K = 2816

BF16 = jnp.bfloat16
F32 = jnp.float32


def _params(n_axes):
    return pltpu.CompilerParams(
        dimension_semantics=("arbitrary",) * n_axes,
        vmem_limit_bytes=V7X_VMEM_LIMIT_BYTES,
    )


def _tile(dim, target):
    if dim <= target:
        return dim
    t = target
    while dim % t:
        t //= 2
    return t


def _cdiv(a, b):
    return -(-a // b)


def _dot(a, b):
    return jnp.dot(a, b, preferred_element_type=F32)


class CastJob(NamedTuple):
    src: jax.Array
    lead: tuple
    tn: int
    rows_out: int


class _JobPlan(NamedTuple):
    rows: int
    cols: int
    tn: int
    panels: int
    rb: int
    nb_src: int
    nb_dst: int
    rows_out: int


def _plan_job(job, n_steps):
    rows, cols = job.src.shape[-2:]
    rb = BF16_SUBLANES
    while _cdiv(job.rows_out, rb) > n_steps:
        rb *= 2
    assert rows % rb == 0 and job.rows_out % rb == 0, (rows, job.rows_out, rb)
    return _JobPlan(rows, cols, job.tn, _cdiv(cols, job.tn), rb, rows // rb, job.rows_out // rb, job.rows_out)


def _job_specs(job, plan, step_of):
    n_lead = len(job.lead)

    def src_idx(*g):
        return (*job.lead, jnp.minimum(step_of(*g), plan.nb_src - 1), 0)

    def dst_idx(*g):
        return (0, jnp.minimum(step_of(*g), plan.nb_dst - 1), 0)

    src_spec = pl.BlockSpec((None,) * n_lead + (plan.rb, plan.cols), src_idx)
    dst_spec = pl.BlockSpec((plan.panels, plan.rb, plan.tn), dst_idx)
    dst_shape = jax.ShapeDtypeStruct((plan.panels, plan.rows_out, plan.tn), BF16)
    return src_spec, dst_spec, dst_shape


def _run_job(plan, src_ref, dst_ref, step):
    v = src_ref[...]
    if plan.rows_out > plan.rows:
        blk = jnp.minimum(step, plan.nb_dst - 1)
        row = blk * plan.rb + lax.broadcasted_iota(jnp.int32, (plan.rb, 1), 0)
        v = jnp.where(row < plan.rows, v, 0.0)
    for p in range(plan.panels):
        lo = p * plan.tn
        width = min(plan.tn, plan.cols - lo)
        dst_ref[p, :, 0:width] = v[:, lo:lo + width].astype(BF16)
        if width < plan.tn:
            dst_ref[p, :, width:plan.tn] = jnp.zeros((plan.rb, plan.tn - width), BF16)


def _call_with_jobs(body, *, name, grid, in_specs, out_specs, out_shape, scratch_shapes, args, jobs):
    n_steps = 1
    for gdim in grid:
        n_steps *= gdim

    def step_of(*g):
        s = g[0]
        for gdim, gi in zip(grid[1:], g[1:]):
            s = s * gdim + gi
        return s

    plans = [_plan_job(j, n_steps) for j in jobs]
    specs = [_job_specs(j, p, step_of) for j, p in zip(jobs, plans)]
    n_in, n_out, n_job = len(in_specs), len(out_specs), len(jobs)

    def kern(*refs):
        ins = refs[:n_in]
        job_src = refs[n_in:n_in + n_job]
        outs = refs[n_in + n_job:n_in + n_job + n_out]
        job_dst = refs[n_in + n_job + n_out:n_in + 2 * n_job + n_out]
        scratch = refs[n_in + 2 * n_job + n_out:]

        def run_jobs():
            if n_job:
                step = step_of(*[pl.program_id(a) for a in range(len(grid))])
                for plan, s_ref, d_ref in zip(plans, job_src, job_dst):
                    _run_job(plan, s_ref, d_ref, step)

        body(run_jobs, *ins, *outs, *scratch)

    res = pl.pallas_call(
        kern,
        grid=grid,
        in_specs=list(in_specs) + [s[0] for s in specs],
        out_specs=list(out_specs) + [s[1] for s in specs],
        out_shape=list(out_shape) + [s[2] for s in specs],
        scratch_shapes=scratch_shapes,
        compiler_params=_params(len(grid)),
        name=name,
    )(*args, *[j.src for j in jobs])
    return list(res[:n_out]), list(res[n_out:])


def cast_panels(job):
    n_steps = _cdiv(job.rows_out, 2 * BF16_SUBLANES)
    _, copies = _call_with_jobs(lambda run_jobs: run_jobs(), name="cast_panels", grid=(n_steps,), in_specs=[],
                                out_specs=[], out_shape=[], scratch_shapes=[], args=[], jobs=[job])
    return copies[0]


def _norm_rows_into(h_ref, row0, x_ref, g_ref):
    g = g_ref[...]
    chunk = BF16_SUBLANES

    def body(c, carry):
        r = pl.multiple_of(c * chunk, chunk)
        x = x_ref[pl.ds(r, chunk), :]
        y = x * lax.rsqrt(jnp.mean(x * x, axis=-1, keepdims=True) + NORM_EPS) * g
        h_ref[pl.ds(row0 + r, chunk), :] = y.astype(BF16)
        return carry

    n_chunks = x_ref.shape[0] // chunk
    lax.fori_loop(0, n_chunks, body, 0, unroll=min(n_chunks, 8))


class _Extra(NamedTuple):
    array: jax.Array
    cols: int
    by_panel: bool
    by_row: bool
    sample: object


def _norm_matmul(name, epilogue, x, g, panels, extras, outs, *, tn, n_panels, x_sample, jobs=()):
    m, d = x.shape
    ms = x_sample.shape[0]
    group = _tile(m, GROUP_ROWS)
    tm = _tile(group, ROW_TILE)
    xb = _tile(group, NORM_ROWS)
    pre = group // xb
    n_groups = m // group
    last = n_groups - 1
    grid = (n_groups, pre + n_panels)
    n_p, n_e, n_o = len(panels), len(extras), len(outs)
    row_extras = [k for k, e in enumerate(extras) if e.by_row]

    def panel_of(n):
        return jnp.maximum(n - pre, 0)

    def sample_panel_of(i, n):
        return jnp.where(i == last, panel_of(n), 0)

    in_specs = [
        pl.BlockSpec((xb, d), lambda i, n: (i * pre + jnp.minimum(n, pre - 1), 0)),
        pl.BlockSpec((1, d), lambda i, n: (0, 0)),
    ]
    args = [x, g]
    for w, off in panels:
        in_specs.append(pl.BlockSpec((None, d, tn), functools.partial(lambda i, n, o: (panel_of(n) + o, 0, 0), o=off)))
        args.append(w)
    for e in extras:
        rows = group if e.by_row else e.array.shape[0]
        in_specs.append(pl.BlockSpec(
            (rows, e.cols),
            functools.partial(lambda i, n, r, c: (i if r else 0, panel_of(n) if c else 0), r=e.by_row, c=e.by_panel)))
        args.append(e.array)
    in_specs.append(pl.BlockSpec((ms, d), lambda i, n: (0, 0)))
    args.append(x_sample)
    for k in row_extras:
        e = extras[k]
        in_specs.append(pl.BlockSpec(
            (ms, e.cols), functools.partial(lambda i, n, c: (0, sample_panel_of(i, n) if c else 0), c=e.by_panel)))
        args.append(e.sample)
    n_in = len(in_specs)

    out_specs = [pl.BlockSpec((group, tn), lambda i, n: (i, panel_of(n))) for _ in outs]
    out_specs += [pl.BlockSpec((ms, tn), lambda i, n: (0, sample_panel_of(i, n))) for _ in outs]
    out_shape = [jax.ShapeDtypeStruct((m, cols), dt) for cols, dt in outs]
    out_shape += [jax.ShapeDtypeStruct((ms, cols), dt) for cols, dt in outs]

    def body(run_jobs, *refs):
        x_ref, g_ref = refs[:2]
        panel_refs = refs[2:2 + n_p]
        extra_refs = refs[2 + n_p:2 + n_p + n_e]
        xs_ref = refs[2 + n_p + n_e]
        extra_s_refs = dict(zip(row_extras, refs[3 + n_p + n_e:n_in]))
        out_refs = refs[n_in:n_in + n_o]
        out_s_refs = refs[n_in + n_o:n_in + 2 * n_o]
        h_ref = refs[n_in + 2 * n_o]
        i, n = pl.program_id(0), pl.program_id(1)

        @pl.when(n < pre)
        def _():
            _norm_rows_into(h_ref, pl.multiple_of(n * xb, xb), x_ref, g_ref)
            run_jobs()

        @pl.when((n == 0) & (i == last))
        def _():
            _norm_rows_into(h_ref, group, xs_ref, g_ref)

        @pl.when(n >= pre)
        def _():
            for s in range(group // tm):
                rows = slice(s * tm, (s + 1) * tm)

                def get_extra(k, rows=rows):
                    return extra_refs[k][rows, :] if extras[k].by_row else extra_refs[k][...]

                def put_out(k, v, rows=rows):
                    out_refs[k][rows, :] = v

                epilogue(h_ref[rows, :], panel_refs, get_extra, put_out)
            run_jobs()

        @pl.when((n >= pre) & (i == last))
        def _():
            def get_extra(k):
                return extra_s_refs[k][...] if extras[k].by_row else extra_refs[k][...]

            def put_out(k, v):
                out_s_refs[k][...] = v

            epilogue(h_ref[group:group + ms, :], panel_refs, get_extra, put_out)

    outs_all, copies = _call_with_jobs(
        body, name=name, grid=grid, in_specs=in_specs, out_specs=out_specs, out_shape=out_shape,
        scratch_shapes=[pltpu.VMEM((group + ms, d), BF16)], args=args, jobs=list(jobs))
    return outs_all[:n_o], outs_all[n_o:], copies


def _swiglu_epilogue(h, panel_refs, get_extra, put_out):
    gate = _dot(h, panel_refs[0][...])
    up = _dot(h, panel_refs[1][...])
    put_out(0, (gate * jax.nn.sigmoid(gate) * up).astype(BF16))


def swiglu_up(x, x_sample, g, wg, wu, *, jobs=()):
    n_panels, _, tn = wg.shape
    outs, outs_s, copies = _norm_matmul(
        "swiglu_up", _swiglu_epilogue, x, g, [(wg, 0), (wu, 0)], [], [(n_panels * tn, BF16)],
        tn=tn, n_panels=n_panels, x_sample=x_sample, jobs=jobs)
    return outs[0], outs_s[0], copies


def _linear_epilogue(h, panel_refs, get_extra, put_out):
    put_out(0, _dot(h, panel_refs[0][...]) + get_extra(0))


def norm_linear(x, x_sample, g, w, b, *, jobs=()):
    n_panels, _, tn = w.shape
    outs, outs_s, copies = _norm_matmul(
        "norm_linear", _linear_epilogue, x, g, [(w, 0)], [_Extra(b, tn, True, False, None)],
        [(n_panels * tn, F32)], tn=tn, n_panels=n_panels, x_sample=x_sample, jobs=jobs)
    return outs[0], outs_s[0], copies


def _conv_gates_epilogue(h, panel_refs, get_extra, put_out):
    put_out(0, _dot(h, panel_refs[0][...]))
    put_out(1, _dot(h, panel_refs[1][...]) * _dot(h, panel_refs[2][...]))


def conv_gates(x, x_sample, g, w_in, *, jobs=()):
    d = x.shape[1]
    tn = w_in.shape[2]
    n_panels = d // tn
    outs, outs_s, copies = _norm_matmul(
        "conv_gates", _conv_gates_epilogue, x, g, [(w_in, 0), (w_in, n_panels), (w_in, 2 * n_panels)], [],
        [(d, F32), (d, F32)], tn=tn, n_panels=n_panels, x_sample=x_sample, jobs=jobs)
    return outs, outs_s, copies


def _ple_epilogue(h, panel_refs, get_extra, put_out):
    gate = jax.nn.sigmoid(_dot(h, panel_refs[0][...]))
    proj = _dot(get_extra(1).astype(BF16), get_extra(2))
    put_out(0, get_extra(0) + gate * proj)


def ple(x, x_sample, g, w_gate, p, p_sample, w_proj, *, jobs=()):
    d = x.shape[1]
    pd = p.shape[1]
    n_panels, _, tn = w_gate.shape
    extras = [
        _Extra(x, tn, True, True, x_sample),
        _Extra(p, pd, False, True, p_sample),
        _Extra(w_proj, tn, True, False, None),
    ]
    outs, outs_s, copies = _norm_matmul(
        "ple", _ple_epilogue, x, g, [(w_gate, 0)], extras, [(d, F32)],
        tn=tn, n_panels=n_panels, x_sample=x_sample, jobs=jobs)
    return outs[0], outs_s[0], copies


def _residual_update(a_ref, w_ref, x_ref, b_ref, o_ref, acc_ref, *, scale, n_k, k, first_step):
    tn = o_ref.shape[1]
    halves = [slice(c * (tn // 2), (c + 1) * (tn // 2)) for c in range(2)] if tn % 256 == 0 else [slice(0, tn)]

    def finish(y, cols):
        if b_ref is not None:
            y = y + b_ref[:, cols]
        return x_ref[:, cols] + scale * y

    if n_k == 1:
        for cols in halves:
            o_ref[:, cols] = finish(_dot(a_ref[...], w_ref[:, cols]), cols)
        return

    @pl.when(first_step)
    def _():
        acc_ref[...] = jnp.zeros_like(acc_ref)

    for cols in halves:
        acc = acc_ref[:, cols] + _dot(a_ref[...], w_ref[:, cols])
        acc_ref[:, cols] = jnp.where(k == n_k - 1, 0.0, acc)
        o_ref[:, cols] = finish(acc, cols)


def _matmul_residual_body(run_jobs, *refs, scale, has_bias, n_k, last_tile):
    a_ref, w_ref, x_ref, as_ref, xs_ref = refs[:5]
    b_ref = refs[5] if has_bias else None
    o_ref, os_ref = refs[5 + has_bias:7 + has_bias]
    acc_ref, accs_ref = refs[7 + has_bias:9 + has_bias] if n_k > 1 else (None, None)
    i, j, k = pl.program_id(0), pl.program_id(1), pl.program_id(2)
    origin = (j == 0) & (k == 0)
    _residual_update(a_ref, w_ref, x_ref, b_ref, o_ref, acc_ref, scale=scale, n_k=n_k, k=k,
                     first_step=origin & (i == 0))
    run_jobs()

    @pl.when(i == last_tile)
    def _():
        _residual_update(as_ref, w_ref, xs_ref, b_ref, os_ref, accs_ref, scale=scale, n_k=n_k, k=k,
                         first_step=origin)


def matmul_residual(a, a_sample, w, x, x_sample, b=None, *, scale, tk, jobs=()):
    m, kdim = a.shape
    ms = a_sample.shape[0]
    n_panels, kw, tn = w.shape
    tm = _tile(m, DOWN_ROWS)
    assert kw == kdim and kdim % tk == 0 and m % tm == 0
    n_k = kdim // tk
    last_tile = m // tm - 1

    def sample_col(i, j):
        return jnp.where(i == last_tile, j, 0)

    in_specs = [
        pl.BlockSpec((tm, tk), lambda i, j, k: (i, k)),
        pl.BlockSpec((None, tk, tn), lambda i, j, k: (j, k, 0)),
        pl.BlockSpec((tm, tn), lambda i, j, k: (i, j)),
        pl.BlockSpec((ms, tk), lambda i, j, k: (0, k)),
        pl.BlockSpec((ms, tn), lambda i, j, k: (0, sample_col(i, j))),
    ]
    args = [a, w, x, a_sample, x_sample]
    if b is not None:
        in_specs.append(pl.BlockSpec((1, tn), lambda i, j, k: (0, j)))
        args.append(b)
    scratch = [pltpu.VMEM((tm, tn), F32), pltpu.VMEM((ms, tn), F32)] if n_k > 1 else []
    outs, copies = _call_with_jobs(
        functools.partial(_matmul_residual_body, scale=scale, has_bias=b is not None, n_k=n_k, last_tile=last_tile),
        name="matmul_residual", grid=(m // tm, n_panels, n_k), in_specs=in_specs,
        out_specs=[pl.BlockSpec((tm, tn), lambda i, j, k: (i, j)),
                   pl.BlockSpec((ms, tn), lambda i, j, k: (0, sample_col(i, j)))],
        out_shape=[jax.ShapeDtypeStruct((m, n_panels * tn), F32), jax.ShapeDtypeStruct((ms, n_panels * tn), F32)],
        scratch_shapes=scratch, args=args, jobs=list(jobs))
    return outs[0], outs[1], copies


def _rmsnorm_kernel(x_ref, g_ref, o_ref):
    x = x_ref[...]
    o_ref[...] = x * lax.rsqrt(jnp.mean(x * x, axis=-1, keepdims=True) + NORM_EPS) * g_ref[...]


def rmsnorm(x, g, *, tm):
    m, d = x.shape
    return pl.pallas_call(
        _rmsnorm_kernel,
        grid=(m // tm,),
        in_specs=[pl.BlockSpec((tm, d), lambda i: (i, 0)), pl.BlockSpec((1, d), lambda i: (0, 0))],
        out_specs=pl.BlockSpec((tm, d), lambda i: (i, 0)),
        out_shape=jax.ShapeDtypeStruct((m, d), F32),
        compiler_params=_params(1),
        name="rmsnorm",
    )(x, g)


def _attn_prompt_body(sink_ref, q_ref, kp_ref, kc_ref, vp_ref, vc_ref, o_ref, *, n_kv, group, scale):
    blk = pl.program_id(1)
    t = WINDOW
    pairs = group // 2
    q = q_ref[...].astype(BF16)
    k_all = jnp.concatenate([kp_ref[...], kc_ref[...]], axis=0)
    v_all = jnp.concatenate([vp_ref[...], vc_ref[...]], axis=0)

    r = lax.broadcasted_iota(jnp.int32, (t, 2 * t), 0)
    c = lax.broadcasted_iota(jnp.int32, (t, 2 * t), 1)
    first_col = jnp.where(blk > 0, r, t)
    valid = (c >= first_col) & (c <= r + t)
    lane = lax.broadcasted_iota(jnp.int32, (2 * t, 2 * HEAD_DIM), 1)
    low = lane < HEAD_DIM

    for n in range(n_kv):
        slab = slice((n // 2) * 2 * HEAD_DIM, (n // 2 + 1) * 2 * HEAD_DIM)
        k2 = k_all[:, slab]
        v2 = v_all[:, slab]
        if n % 2 == 0:
            k_lo = jnp.where(low, k2, 0.0)
            v_lo = jnp.where(low, v2, 0.0)
            k_hi = pltpu.roll(k_lo, HEAD_DIM, 1)
            v_hi = pltpu.roll(v_lo, HEAD_DIM, 1)
        else:
            k_hi = jnp.where(low, 0.0, k2)
            v_hi = jnp.where(low, 0.0, v2)
            k_lo = pltpu.roll(k_hi, HEAD_DIM, 1)
            v_lo = pltpu.roll(v_hi, HEAD_DIM, 1)
        k_halves = (k_lo.astype(BF16), k_hi.astype(BF16))
        v_halves = (v_lo.astype(BF16), v_hi.astype(BF16))

        q_slabs = [q[:, (n * pairs + j) * 2 * HEAD_DIM:(n * pairs + j + 1) * 2 * HEAD_DIM] for j in range(pairs)]
        q4 = jnp.concatenate(q_slabs, axis=0)
        out = None
        for half in range(2):
            s = lax.dot_general(q4, k_halves[half], (((1,), (1,)), ((), ())),
                                preferred_element_type=F32) * scale
            probs = []
            for j in range(pairs):
                sink = sink_ref[n * group + 2 * j + half]
                sj = jnp.where(valid, s[j * t:(j + 1) * t], NEG_INF)
                m = jnp.maximum(jnp.max(sj, axis=-1, keepdims=True), sink)
                pj = jnp.exp(sj - m)
                pj = pj / (jnp.sum(pj, axis=-1, keepdims=True) + jnp.exp(sink - m))
                probs.append(pj.astype(BF16))
            part = _dot(jnp.concatenate(probs, axis=0), v_halves[half])
            out = part if out is None else out + part
        for j in range(pairs):
            col = (n * pairs + j) * 2 * HEAD_DIM
            o_ref[:, col:col + 2 * HEAD_DIM] = out[j * t:(j + 1) * t].astype(o_ref.dtype)


def attn_prompt(z, sinks, *, batch, seq, n_heads, n_kv):
    t = WINDOW
    nb = seq // t
    hq = n_heads * HEAD_DIM
    hk = n_kv * HEAD_DIM
    kcol = hq // hk
    vcol = kcol + 1

    def cur(b, i, s):
        return b * nb + i

    def prev(b, i, s):
        return b * nb + jnp.maximum(i - 1, 0)

    grid_spec = pltpu.PrefetchScalarGridSpec(
        num_scalar_prefetch=1,
        grid=(batch, nb),
        in_specs=[
            pl.BlockSpec((t, hq), lambda b, i, s: (cur(b, i, s), 0)),
            pl.BlockSpec((t, hk), lambda b, i, s: (prev(b, i, s), kcol)),
            pl.BlockSpec((t, hk), lambda b, i, s: (cur(b, i, s), kcol)),
            pl.BlockSpec((t, hk), lambda b, i, s: (prev(b, i, s), vcol)),
            pl.BlockSpec((t, hk), lambda b, i, s: (cur(b, i, s), vcol)),
        ],
        out_specs=pl.BlockSpec((t, hq), lambda b, i, s: (cur(b, i, s), 0)),
    )
    return pl.pallas_call(
        functools.partial(_attn_prompt_body, n_kv=n_kv, group=n_heads // n_kv, scale=HEAD_DIM ** -0.5),
        grid_spec=grid_spec,
        out_shape=jax.ShapeDtypeStruct((batch * seq, hq), BF16),
        compiler_params=_params(2),
        name="attn_prompt",
    )(sinks, z, z, z, z, z)


def _attn_sample_body(q_ref, kc_ref, vc_ref, kn_ref, vn_ref, sink_ref, o_ref, *, n_kv, group, scale, first_valid):
    q = q_ref[...]
    for n in range(n_kv):
        lanes = slice(n * HEAD_DIM, (n + 1) * HEAD_DIM)
        rows = slice(n * group, (n + 1) * group)
        qn = q[rows]
        kc = kc_ref[:, lanes]
        vc = vc_ref[:, lanes]
        kn = kn_ref[:, lanes]
        vn = vn_ref[:, lanes]
        sink = sink_ref[rows]
        s_c = lax.dot_general(qn.astype(BF16), kc.astype(BF16), (((1,), (1,)), ((), ())),
                              preferred_element_type=F32) * scale
        qb = qn.astype(BF16).astype(F32)
        kb = kn.astype(BF16).astype(F32)
        s_n = jnp.sum(qb * kb, axis=-1, keepdims=True) * scale
        if first_valid > 0:
            col = lax.broadcasted_iota(jnp.int32, s_c.shape, 1)
            s_c = jnp.where(col >= first_valid, s_c, NEG_INF)
        m = jnp.maximum(jnp.maximum(jnp.max(s_c, axis=-1, keepdims=True), s_n), sink)
        p_c = jnp.exp(s_c - m)
        p_n = jnp.exp(s_n - m)
        denom = jnp.sum(p_c, axis=-1, keepdims=True) + p_n + jnp.exp(sink - m)
        p_c = p_c / denom
        p_n = p_n / denom
        o = _dot(p_c.astype(BF16), vc.astype(BF16))
        o = o + p_n.astype(BF16).astype(F32) * vn.astype(BF16).astype(F32)
        o_ref[rows, :] = o


def attn_sample(q, kc, vc, kn, vn, sinks, *, n_kv):
    nseq, n_heads, hd = q.shape
    wb = kc.shape[1]
    hk = kc.shape[2]
    first_valid = max(0, wb - WINDOW, wb - PAST_LEN)
    return pl.pallas_call(
        functools.partial(_attn_sample_body, n_kv=n_kv, group=n_heads // n_kv, scale=HEAD_DIM ** -0.5,
                          first_valid=first_valid),
        grid=(nseq,),
        in_specs=[
            pl.BlockSpec((None, n_heads, hd), lambda b: (b, 0, 0)),
            pl.BlockSpec((None, wb, hk), lambda b: (b, 0, 0)),
            pl.BlockSpec((None, wb, hk), lambda b: (b, 0, 0)),
            pl.BlockSpec((None, 1, hk), lambda b: (b, 0, 0)),
            pl.BlockSpec((None, 1, hk), lambda b: (b, 0, 0)),
            pl.BlockSpec((n_heads, 1), lambda b: (0, 0)),
        ],
        out_specs=pl.BlockSpec((None, n_heads, hd), lambda b: (b, 0, 0)),
        out_shape=jax.ShapeDtypeStruct((nseq, n_heads, hd), F32),
        compiler_params=_params(1),
        name="attn_sample",
    )(q, kc, vc, kn, vn, sinks)


def _conv_mix_prompt_body(gb_ref, cu_ref, halo_ref, w_ref, o_ref, pad_ref, *, tiles_per_seq):
    tm = cu_ref.shape[0]
    first = (pl.program_id(0) % tiles_per_seq) == 0
    halo = jnp.where(first, 0.0, halo_ref[...])
    pad_ref[0:8, :] = halo
    pad_ref[8:8 + tm, :] = cu_ref[...]
    w = w_ref[...]
    dw = w[0:1] * pad_ref[6:6 + tm, :] + w[1:2] * pad_ref[7:7 + tm, :] + w[2:3] * pad_ref[8:8 + tm, :]
    o_ref[...] = (gb_ref[...] * dw).astype(o_ref.dtype)


def conv_mix_prompt(gb, cu, w, *, seq, tm, tn):
    m, d = gb.shape
    hb = tm // 8
    return pl.pallas_call(
        functools.partial(_conv_mix_prompt_body, tiles_per_seq=seq // tm),
        grid=(m // tm, d // tn),
        in_specs=[
            pl.BlockSpec((tm, tn), lambda i, j: (i, j)),
            pl.BlockSpec((tm, tn), lambda i, j: (i, j)),
            pl.BlockSpec((8, tn), lambda i, j: (jnp.maximum(i * hb - 1, 0), j)),
            pl.BlockSpec((CONV_WIDTH, tn), lambda i, j: (0, j)),
        ],
        out_specs=pl.BlockSpec((tm, tn), lambda i, j: (i, j)),
        out_shape=jax.ShapeDtypeStruct((m, d), BF16),
        scratch_shapes=[pltpu.VMEM((tm + 8, tn), F32)],
        compiler_params=_params(2),
        name="conv_mix_prompt",
    )(gb, cu, cu, w)


def _conv_mix_sample_body(gb_ref, cu_ref, s0_ref, s1_ref, w_ref, o_ref):
    w = w_ref[...]
    dw = w[0:1] * s0_ref[...] + w[1:2] * s1_ref[...] + w[2:3] * cu_ref[...]
    o_ref[...] = (gb_ref[...] * dw).astype(o_ref.dtype)


def conv_mix_sample(gb, cu, s0, s1, w):
    return pl.pallas_call(
        _conv_mix_sample_body,
        out_shape=jax.ShapeDtypeStruct(gb.shape, BF16),
        name="conv_mix_sample",
    )(gb, cu, s0, s1, w)


def kernel(x_prompt, x_sample, cache_k, cache_v, state_conv, p_prompt, p_sample, norm_gains, final_norm_gain,
           w_ffn_gate, w_ffn_up, w_ffn_down, w_qkv, b_qkv, attn_sinks, w_o, b_o, w_conv_in, conv_w, w_conv_out,
           w_ple_proj, w_ple_gate):
    batch, seq, d = x_prompt.shape
    nseq, dec_seq, _ = x_sample.shape
    assert dec_seq == 1 and seq % WINDOW == 0 and CONV_WIDTH == conv_w.shape[1]
    depth = norm_gains.shape[0]
    d_ff = w_ffn_gate.shape[-1]
    n_heads = attn_sinks.shape[1]
    qkv_dim = w_qkv.shape[-1]
    n_kv = (qkv_dim // HEAD_DIM - n_heads) // 2
    hq, hk = n_heads * HEAD_DIM, n_kv * HEAD_DIM
    wb = cache_k.shape[2]
    mp, ms = batch * seq, nseq

    f_pad = _cdiv(d_ff, PANEL_FFN) * PANEL_FFN
    tk_d = _tile(f_pad, DOWN_K)
    panel_qkv = _tile(qkv_dim, PANEL_QKV)
    panel_conv = _tile(d, PANEL_CONV)
    panel_ple = _tile(d, PANEL_PLE)
    panel_out = _tile(d, PANEL_OUT)

    def job_gate(i, h):
        return CastJob(w_ffn_gate, (i, h), PANEL_FFN, d)

    def job_up(i, h):
        return CastJob(w_ffn_up, (i, h), PANEL_FFN, d)

    def job_down(i, h):
        return CastJob(w_ffn_down, (i, h), panel_out, f_pad)

    def job_mixer_in(i):
        if i % 2 == 0:
            return CastJob(w_qkv, (i // 2,), panel_qkv, d)
        return CastJob(w_conv_in, (i // 2,), panel_conv, d)

    def job_mixer_out(i):
        if i % 2 == 0:
            return CastJob(w_o, (i // 2,), panel_out, hq)
        return CastJob(w_conv_out, (i // 2,), panel_out, d)

    def job_ple_gate(i):
        return CastJob(w_ple_gate, (i,), panel_ple, d)

    wpp = w_ple_proj.astype(BF16)

    xp = x_prompt.reshape(mp, d)
    xs = x_sample.reshape(ms, d)
    pp = p_prompt.reshape(depth, mp, -1)
    ps = p_sample.reshape(depth, ms, -1)

    wg = cast_panels(job_gate(0, 0))
    wu = cast_panels(job_up(0, 0))

    nkp, nvp, nks, nvs, ncp, ncs = [], [], [], [], [], []
    for i in range(depth):
        g = norm_gains[i][:, None, :]
        last = i == depth - 1
        j = i // 2

        act, act_s, (wd, w_mi, wu_next) = swiglu_up(
            xp, xs, g[0], wg, wu, jobs=[job_down(i, 0), job_mixer_in(i), job_up(i, 1)])
        xp, xs, (w_mo,) = matmul_residual(act, act_s, wd, xp, xs, scale=0.5, tk=tk_d, jobs=[job_mixer_out(i)])

        if i % 2 == 0:
            zp, zs, (wg_next,) = norm_linear(xp, xs, g[1], w_mi, b_qkv[j][None, :], jobs=[job_gate(i, 1)])
            op = attn_prompt(zp, attn_sinks[j], batch=batch, seq=seq, n_heads=n_heads, n_kv=n_kv)
            kn = zs[:, hq:hq + hk]
            vn = zs[:, hq + hk:]
            os_ = attn_sample(zs[:, :hq].reshape(ms, n_heads, HEAD_DIM),
                              cache_k[j].reshape(ms, wb, hk), cache_v[j].reshape(ms, wb, hk),
                              kn[:, None, :], vn[:, None, :], attn_sinks[j][:, None],
                              n_kv=n_kv)
            os_ = os_.reshape(ms, hq).astype(BF16)
            xp, xs, _ = matmul_residual(op, os_, w_mo, xp, xs, b_o[j][None, :], scale=1.0, tk=hq)
            zp3 = zp.reshape(batch, seq, qkv_dim)
            wp_ = min(WINDOW, seq)
            nkp.append(zp3[:, seq - wp_:, hq:hq + hk].reshape(batch, wp_, n_kv, HEAD_DIM))
            nvp.append(zp3[:, seq - wp_:, hq + hk:].reshape(batch, wp_, n_kv, HEAD_DIM))
            kk = jnp.concatenate([cache_k[j], kn.reshape(ms, 1, n_kv, HEAD_DIM)], axis=1)
            vv = jnp.concatenate([cache_v[j], vn.reshape(ms, 1, n_kv, HEAD_DIM)], axis=1)
            nks.append(kk[:, -wb:])
            nvs.append(vv[:, -wb:])
        else:
            (gbp, cup), (gbs, cus), (wg_next,) = conv_gates(xp, xs, g[1], w_mi, jobs=[job_gate(i, 1)])
            tp = conv_mix_prompt(gbp, cup, conv_w[j], seq=seq, tm=_tile(seq, 512), tn=_tile(d, 1024))
            st = state_conv[j]
            ts = conv_mix_sample(gbs, cus, st[:, 0], st[:, 1], conv_w[j])
            xp, xs, _ = matmul_residual(tp, ts, w_mo, xp, xs, scale=1.0, tk=d)
            ncp.append(cup.reshape(batch, seq, d)[:, seq - (CONV_WIDTH - 1):])
            ncs.append(jnp.concatenate([st, cus[:, None, :]], axis=1)[:, -(CONV_WIDTH - 1):])

        wg, wu = wg_next, wu_next
        act, act_s, (wd, w_pg) = swiglu_up(xp, xs, g[2], wg, wu, jobs=[job_down(i, 1), job_ple_gate(i)])
        xp, xs, copies = matmul_residual(act, act_s, wd, xp, xs, scale=0.5, tk=tk_d,
                                         jobs=[] if last else [job_gate(i + 1, 0)])
        if not last:
            wg = copies[0]

        xp, xs, copies = ple(xp, xs, g[3], w_pg, pp[i], ps[i], wpp[i], jobs=[] if last else [job_up(i + 1, 0)])
        if not last:
            wu = copies[0]

    gf = final_norm_gain[None, :]
    y_prompt = rmsnorm(xp, gf, tm=_tile(mp, 256)).reshape(batch, seq, d)
    y_sample = rmsnorm(xs, gf, tm=ms).reshape(nseq, dec_seq, d)
    return (y_prompt, y_sample, jnp.stack(nkp), jnp.stack(nvp), jnp.stack(nks), jnp.stack(nvs),
            jnp.stack(ncp), jnp.stack(ncs))
```

```python
import functools
import math
from typing import NamedTuple

import jax
import jax.numpy as jnp
from jax import lax
from jax.experimental import pallas as pl
from jax.experimental.pallas import tpu as pltpu

NORM_EPS = 1e-6
NEG_INF = -1e30
HEAD_DIM = 64
WINDOW = 128
CONV_WIDTH = 3
PAST_LEN = 8192

V7X_VMEM_LIMIT_BYTES = 58 * 1024 * 1024
BF16_SUBLANES = 16

GROUP_ROWS = 1024
ROW_TILE = 512
NORM_ROWS = 256
PANEL_FFN = 512
PANEL_QKV = 512
PANEL_CONV = 256
PANEL_PLE = 512
PANEL_OUT = 1024
DOWN_ROWS = 1024
DOWN_K = 2816
SAMPLE_SEQS = 8

BF16 = jnp.bfloat16
F32 = jnp.float32


def _params(n_axes):
    return pltpu.CompilerParams(
        dimension_semantics=("arbitrary",) * n_axes,
        vmem_limit_bytes=V7X_VMEM_LIMIT_BYTES,
    )


def _tile(dim, target):
    if dim <= target:
        return dim
    t = target
    while dim % t:
        t //= 2
    return t


def _cdiv(a, b):
    return -(-a // b)


def _dot(a, b):
    return jnp.dot(a, b, preferred_element_type=F32)


class CastJob(NamedTuple):
    src: jax.Array
    lead: tuple
    tn: int
    rows_out: int


class _JobPlan(NamedTuple):
    rows: int
    cols: int
    tn: int
    panels: int
    rb: int
    nb_src: int
    nb_dst: int
    rows_out: int


def _plan_job(job, n_steps):
    rows, cols = job.src.shape[-2:]
    rb = BF16_SUBLANES
    while _cdiv(job.rows_out, rb) > n_steps:
        rb *= 2
    assert rows % rb == 0 and job.rows_out % rb == 0, (rows, job.rows_out, rb)
    return _JobPlan(rows, cols, job.tn, _cdiv(cols, job.tn), rb, rows // rb, job.rows_out // rb, job.rows_out)


def _job_specs(job, plan, step_of):
    n_lead = len(job.lead)

    def src_idx(*g):
        return (*job.lead, jnp.minimum(step_of(*g), plan.nb_src - 1), 0)

    def dst_idx(*g):
        return (0, jnp.minimum(step_of(*g), plan.nb_dst - 1), 0)

    src_spec = pl.BlockSpec((None,) * n_lead + (plan.rb, plan.cols), src_idx)
    dst_spec = pl.BlockSpec((plan.panels, plan.rb, plan.tn), dst_idx)
    dst_shape = jax.ShapeDtypeStruct((plan.panels, plan.rows_out, plan.tn), BF16)
    return src_spec, dst_spec, dst_shape


def _run_job(plan, src_ref, dst_ref, step):
    v = src_ref[...]
    if plan.rows_out > plan.rows:
        blk = jnp.minimum(step, plan.nb_dst - 1)
        row = blk * plan.rb + lax.broadcasted_iota(jnp.int32, (plan.rb, 1), 0)
        v = jnp.where(row < plan.rows, v, 0.0)
    for p in range(plan.panels):
        lo = p * plan.tn
        width = min(plan.tn, plan.cols - lo)
        dst_ref[p, :, 0:width] = v[:, lo:lo + width].astype(BF16)
        if width < plan.tn:
            dst_ref[p, :, width:plan.tn] = jnp.zeros((plan.rb, plan.tn - width), BF16)


def _call_with_jobs(body, *, name, grid, in_specs, out_specs, out_shape, scratch_shapes, args, jobs):
    n_steps = 1
    for gdim in grid:
        n_steps *= gdim

    def step_of(*g):
        s = g[0]
        for gdim, gi in zip(grid[1:], g[1:]):
            s = s * gdim + gi
        return s

    plans = [_plan_job(j, n_steps) for j in jobs]
    specs = [_job_specs(j, p, step_of) for j, p in zip(jobs, plans)]
    n_in, n_out, n_job = len(in_specs), len(out_specs), len(jobs)

    def kern(*refs):
        ins = refs[:n_in]
        job_src = refs[n_in:n_in + n_job]
        outs = refs[n_in + n_job:n_in + n_job + n_out]
        job_dst = refs[n_in + n_job + n_out:n_in + 2 * n_job + n_out]
        scratch = refs[n_in + 2 * n_job + n_out:]

        def run_jobs():
            if n_job:
                step = step_of(*[pl.program_id(a) for a in range(len(grid))])
                for plan, s_ref, d_ref in zip(plans, job_src, job_dst):
                    _run_job(plan, s_ref, d_ref, step)

        body(run_jobs, *ins, *outs, *scratch)

    res = pl.pallas_call(
        kern,
        grid=grid,
        in_specs=list(in_specs) + [s[0] for s in specs],
        out_specs=list(out_specs) + [s[1] for s in specs],
        out_shape=list(out_shape) + [s[2] for s in specs],
        scratch_shapes=scratch_shapes,
        compiler_params=_params(len(grid)),
        name=name,
    )(*args, *[j.src for j in jobs])
    return list(res[:n_out]), list(res[n_out:])


def cast_panels(job):
    n_steps = _cdiv(job.rows_out, 2 * BF16_SUBLANES)
    _, copies = _call_with_jobs(lambda run_jobs: run_jobs(), name="cast_panels", grid=(n_steps,), in_specs=[],
                                out_specs=[], out_shape=[], scratch_shapes=[], args=[], jobs=[job])
    return copies[0]


def _norm_rows_into(h_ref, row0, x_ref, g_ref):
    g = g_ref[...]
    chunk = BF16_SUBLANES

    def body(c, carry):
        r = pl.multiple_of(c * chunk, chunk)
        x = x_ref[pl.ds(r, chunk), :]
        y = x * lax.rsqrt(jnp.mean(x * x, axis=-1, keepdims=True) + NORM_EPS) * g
        h_ref[pl.ds(row0 + r, chunk), :] = y.astype(BF16)
        return carry

    n_chunks = x_ref.shape[0] // chunk
    lax.fori_loop(0, n_chunks, body, 0, unroll=min(n_chunks, 8))


class _Extra(NamedTuple):
    array: jax.Array
    cols: int
    by_panel: bool
    by_row: bool
    sample: object


def _norm_matmul(name, epilogue, x, g, panels, extras, outs, *, tn, n_panels, x_sample, last_cols=None, jobs=()):
    m, d = x.shape
    last_cols = tn if last_cols is None else last_cols
    assert last_cols == tn or not any(e.by_panel for e in extras)
    ms = x_sample.shape[0]
    group = _tile(m, GROUP_ROWS)
    tm = _tile(group, ROW_TILE)
    xb = _tile(group, NORM_ROWS)
    pre = group // xb
    n_groups = m // group
    last = n_groups - 1
    grid = (n_groups, pre + n_panels)
    n_p, n_e, n_o = len(panels), len(extras), len(outs)
    row_extras = [k for k, e in enumerate(extras) if e.by_row]

    def panel_of(n):
        return jnp.maximum(n - pre, 0)

    def sample_panel_of(i, n):
        return jnp.where(i == last, panel_of(n), 0)

    in_specs = [
        pl.BlockSpec((xb, d), lambda i, n: (i * pre + jnp.minimum(n, pre - 1), 0)),
        pl.BlockSpec((1, d), lambda i, n: (0, 0)),
    ]
    args = [x, g]
    for w, off in panels:
        in_specs.append(pl.BlockSpec((None, d, tn), functools.partial(lambda i, n, o: (panel_of(n) + o, 0, 0), o=off)))
        args.append(w)
    for e in extras:
        rows = group if e.by_row else e.array.shape[0]
        in_specs.append(pl.BlockSpec(
            (rows, e.cols),
            functools.partial(lambda i, n, r, c: (i if r else 0, panel_of(n) if c else 0), r=e.by_row, c=e.by_panel)))
        args.append(e.array)
    in_specs.append(pl.BlockSpec((ms, d), lambda i, n: (0, 0)))
    args.append(x_sample)
    for k in row_extras:
        e = extras[k]
        in_specs.append(pl.BlockSpec(
            (ms, e.cols), functools.partial(lambda i, n, c: (0, sample_panel_of(i, n) if c else 0), c=e.by_panel)))
        args.append(e.sample)
    n_in = len(in_specs)

    out_specs = [pl.BlockSpec((group, tn), lambda i, n: (i, panel_of(n))) for _ in outs]
    out_specs += [pl.BlockSpec((ms, tn), lambda i, n: (0, sample_panel_of(i, n))) for _ in outs]
    out_shape = [jax.ShapeDtypeStruct((m, cols), dt) for cols, dt in outs]
    out_shape += [jax.ShapeDtypeStruct((ms, cols), dt) for cols, dt in outs]

    def body(run_jobs, *refs):
        x_ref, g_ref = refs[:2]
        panel_refs = refs[2:2 + n_p]
        extra_refs = refs[2 + n_p:2 + n_p + n_e]
        xs_ref = refs[2 + n_p + n_e]
        extra_s_refs = dict(zip(row_extras, refs[3 + n_p + n_e:n_in]))
        out_refs = refs[n_in:n_in + n_o]
        out_s_refs = refs[n_in + n_o:n_in + 2 * n_o]
        h_ref = refs[n_in + 2 * n_o]
        i, n = pl.program_id(0), pl.program_id(1)

        @pl.when(n < pre)
        def _():
            _norm_rows_into(h_ref, pl.multiple_of(n * xb, xb), x_ref, g_ref)
            run_jobs()

        @pl.when((n == 0) & (i == last))
        def _():
            _norm_rows_into(h_ref, group, xs_ref, g_ref)

        def store(ref, rows, v):
            width = v.shape[1]
            ref[rows, 0:width] = v
            if width < tn:
                ref[rows, width:tn] = jnp.zeros((v.shape[0], tn - width), v.dtype)

        def panel_getter(cols):
            return lambda k: panel_refs[k][...] if cols == tn else panel_refs[k][:, 0:cols]

        def prompt_step(cols):
            for s in range(group // tm):
                rows = slice(s * tm, (s + 1) * tm)

                def get_extra(k, rows=rows):
                    return extra_refs[k][rows, :] if extras[k].by_row else extra_refs[k][...]

                def put_out(k, v, rows=rows):
                    store(out_refs[k], rows, v)

                epilogue(h_ref[rows, :], panel_getter(cols), get_extra, put_out)
            run_jobs()

        def sample_step(cols):
            def get_extra(k):
                return extra_s_refs[k][...] if extras[k].by_row else extra_refs[k][...]

            def put_out(k, v):
                store(out_s_refs[k], slice(None), v)

            epilogue(h_ref[group:group + ms, :], panel_getter(cols), get_extra, put_out)

        last_panel = pre + n_panels - 1
        variants = [(n >= pre, tn)] if last_cols == tn else [((n >= pre) & (n < last_panel), tn),
                                                            (n == last_panel, last_cols)]
        for cond, cols in variants:
            pl.when(cond)(functools.partial(prompt_step, cols))
            pl.when(cond & (i == last))(functools.partial(sample_step, cols))

    outs_all, copies = _call_with_jobs(
        body, name=name, grid=grid, in_specs=in_specs, out_specs=out_specs, out_shape=out_shape,
        scratch_shapes=[pltpu.VMEM((group + ms, d), BF16)], args=args, jobs=list(jobs))
    return outs_all[:n_o], outs_all[n_o:], copies


def _swiglu_epilogue(h, panel, get_extra, put_out):
    gate = _dot(h, panel(0))
    up = _dot(h, panel(1))
    put_out(0, (gate * jax.nn.sigmoid(gate) * up).astype(BF16))


def swiglu_up(x, x_sample, g, wg, wu, *, d_ff, jobs=()):
    n_panels, _, tn = wg.shape
    outs, outs_s, copies = _norm_matmul(
        "swiglu_up", _swiglu_epilogue, x, g, [(wg, 0), (wu, 0)], [], [(n_panels * tn, BF16)],
        tn=tn, n_panels=n_panels, x_sample=x_sample, last_cols=d_ff - (n_panels - 1) * tn, jobs=jobs)
    return outs[0], outs_s[0], copies


def _linear_epilogue(h, panel, get_extra, put_out):
    put_out(0, _dot(h, panel(0)) + get_extra(0))


def norm_linear(x, x_sample, g, w, b, *, jobs=()):
    n_panels, _, tn = w.shape
    outs, outs_s, copies = _norm_matmul(
        "norm_linear", _linear_epilogue, x, g, [(w, 0)], [_Extra(b, tn, True, False, None)],
        [(n_panels * tn, F32)], tn=tn, n_panels=n_panels, x_sample=x_sample, jobs=jobs)
    return outs[0], outs_s[0], copies


def _conv_gates_epilogue(h, panel, get_extra, put_out):
    put_out(0, _dot(h, panel(0)))
    put_out(1, _dot(h, panel(1)) * _dot(h, panel(2)))


def conv_gates(x, x_sample, g, w_in, *, jobs=()):
    d = x.shape[1]
    tn = w_in.shape[2]
    n_panels = d // tn
    outs, outs_s, copies = _norm_matmul(
        "conv_gates", _conv_gates_epilogue, x, g, [(w_in, 0), (w_in, n_panels), (w_in, 2 * n_panels)], [],
        [(d, F32), (d, F32)], tn=tn, n_panels=n_panels, x_sample=x_sample, jobs=jobs)
    return outs, outs_s, copies


def _ple_epilogue(h, panel, get_extra, put_out):
    gate = jax.nn.sigmoid(_dot(h, panel(0)))
    proj = _dot(get_extra(1).astype(BF16), get_extra(2))
    put_out(0, get_extra(0) + gate * proj)


def ple(x, x_sample, g, w_gate, p, p_sample, w_proj, *, jobs=()):
    d = x.shape[1]
    pd = p.shape[1]
    n_panels, _, tn = w_gate.shape
    extras = [
        _Extra(x, tn, True, True, x_sample),
        _Extra(p, pd, False, True, p_sample),
        _Extra(w_proj, tn, True, False, None),
    ]
    outs, outs_s, copies = _norm_matmul(
        "ple", _ple_epilogue, x, g, [(w_gate, 0)], extras, [(d, F32)],
        tn=tn, n_panels=n_panels, x_sample=x_sample, jobs=jobs)
    return outs[0], outs_s[0], copies


def _residual_update(a_ref, w_ref, x_ref, b_ref, o_ref, acc_ref, *, scale, n_k, k, first_step, last_k_cols,
                     then=lambda: None):
    tn = o_ref.shape[1]
    tk = a_ref.shape[1]
    halves = [slice(c * (tn // 2), (c + 1) * (tn // 2)) for c in range(2)] if tn % 256 == 0 else [slice(0, tn)]

    def finish(y, cols):
        if b_ref is not None:
            y = y + b_ref[:, cols]
        return x_ref[:, cols] + scale * y

    def product(cols, k_cols):
        if k_cols == tk:
            return _dot(a_ref[...], w_ref[:, cols])
        return _dot(a_ref[:, 0:k_cols], w_ref[0:k_cols, cols])

    if n_k == 1:
        for cols in halves:
            o_ref[:, cols] = finish(product(cols, last_k_cols), cols)
        then()
        return

    @pl.when(first_step)
    def _():
        acc_ref[...] = jnp.zeros_like(acc_ref)

    @pl.when(k < n_k - 1)
    def _():
        for cols in halves:
            acc_ref[:, cols] += product(cols, tk)
        then()

    @pl.when(k == n_k - 1)
    def _():
        for cols in halves:
            acc = acc_ref[:, cols] + product(cols, last_k_cols)
            acc_ref[:, cols] = jnp.zeros_like(acc)
            o_ref[:, cols] = finish(acc, cols)
        then()


def _matmul_residual_body(run_jobs, *refs, scale, has_bias, n_k, last_tile, last_k_cols):
    a_ref, w_ref, x_ref, as_ref, xs_ref = refs[:5]
    b_ref = refs[5] if has_bias else None
    o_ref, os_ref = refs[5 + has_bias:7 + has_bias]
    acc_ref, accs_ref = refs[7 + has_bias:9 + has_bias] if n_k > 1 else (None, None)
    i, j, k = pl.program_id(0), pl.program_id(1), pl.program_id(2)
    origin = (j == 0) & (k == 0)
    _residual_update(a_ref, w_ref, x_ref, b_ref, o_ref, acc_ref, scale=scale, n_k=n_k, k=k,
                     first_step=origin & (i == 0), last_k_cols=last_k_cols, then=run_jobs)

    @pl.when(i == last_tile)
    def _():
        _residual_update(as_ref, w_ref, xs_ref, b_ref, os_ref, accs_ref, scale=scale, n_k=n_k, k=k,
                         first_step=origin, last_k_cols=last_k_cols)


def matmul_residual(a, a_sample, w, x, x_sample, b=None, *, scale, tk, k_valid=None, jobs=()):
    m, kdim = a.shape
    ms = a_sample.shape[0]
    n_panels, kw, tn = w.shape
    tm = _tile(m, DOWN_ROWS)
    assert kw == kdim and kdim % tk == 0 and m % tm == 0
    n_k = kdim // tk
    last_tile = m // tm - 1
    last_k_cols = tk if k_valid is None else k_valid - (n_k - 1) * tk
    assert 0 < last_k_cols <= tk and last_k_cols % 128 == 0

    def sample_col(i, j):
        return jnp.where(i == last_tile, j, 0)

    in_specs = [
        pl.BlockSpec((tm, tk), lambda i, j, k: (i, k)),
        pl.BlockSpec((None, tk, tn), lambda i, j, k: (j, k, 0)),
        pl.BlockSpec((tm, tn), lambda i, j, k: (i, j)),
        pl.BlockSpec((ms, tk), lambda i, j, k: (0, k)),
        pl.BlockSpec((ms, tn), lambda i, j, k: (0, sample_col(i, j))),
    ]
    args = [a, w, x, a_sample, x_sample]
    if b is not None:
        in_specs.append(pl.BlockSpec((1, tn), lambda i, j, k: (0, j)))
        args.append(b)
    scratch = [pltpu.VMEM((tm, tn), F32), pltpu.VMEM((ms, tn), F32)] if n_k > 1 else []
    outs, copies = _call_with_jobs(
        functools.partial(_matmul_residual_body, scale=scale, has_bias=b is not None, n_k=n_k, last_tile=last_tile,
                          last_k_cols=last_k_cols),
        name="matmul_residual", grid=(m // tm, n_panels, n_k), in_specs=in_specs,
        out_specs=[pl.BlockSpec((tm, tn), lambda i, j, k: (i, j)),
                   pl.BlockSpec((ms, tn), lambda i, j, k: (0, sample_col(i, j)))],
        out_shape=[jax.ShapeDtypeStruct((m, n_panels * tn), F32), jax.ShapeDtypeStruct((ms, n_panels * tn), F32)],
        scratch_shapes=scratch, args=args, jobs=list(jobs))
    return outs[0], outs[1], copies


def _rmsnorm_kernel(x_ref, g_ref, o_ref):
    x = x_ref[...]
    o_ref[...] = x * lax.rsqrt(jnp.mean(x * x, axis=-1, keepdims=True) + NORM_EPS) * g_ref[...]


def rmsnorm(x, g, *, tm):
    m, d = x.shape
    return pl.pallas_call(
        _rmsnorm_kernel,
        grid=(m // tm,),
        in_specs=[pl.BlockSpec((tm, d), lambda i: (i, 0)), pl.BlockSpec((1, d), lambda i: (0, 0))],
        out_specs=pl.BlockSpec((tm, d), lambda i: (i, 0)),
        out_shape=jax.ShapeDtypeStruct((m, d), F32),
        compiler_params=_params(1),
        name="rmsnorm",
    )(x, g)


def _attn_prompt_body(sink_ref, q_ref, kp_ref, kc_ref, vp_ref, vc_ref, o_ref, *, n_kv, group, scale):
    blk = pl.program_id(1)
    t = WINDOW
    pairs = group // 2
    q = (q_ref[...] * scale).astype(BF16)
    k_all = jnp.concatenate([kp_ref[...], kc_ref[...]], axis=0)
    v_all = jnp.concatenate([vp_ref[...], vc_ref[...]], axis=0)

    r = lax.broadcasted_iota(jnp.int32, (t, 2 * t), 0)
    c = lax.broadcasted_iota(jnp.int32, (t, 2 * t), 1)
    first_col = jnp.where(blk > 0, r, t)
    valid = (c >= first_col) & (c <= r + t)
    lane = lax.broadcasted_iota(jnp.int32, (2 * t, 2 * HEAD_DIM), 1)
    low = lane < HEAD_DIM

    for n in range(n_kv):
        slab = slice((n // 2) * 2 * HEAD_DIM, (n // 2 + 1) * 2 * HEAD_DIM)
        k2 = k_all[:, slab]
        v2 = v_all[:, slab]
        if n % 2 == 0:
            k_lo = jnp.where(low, k2, 0.0)
            v_lo = jnp.where(low, v2, 0.0)
            k_hi = pltpu.roll(k_lo, HEAD_DIM, 1)
            v_hi = pltpu.roll(v_lo, HEAD_DIM, 1)
        else:
            k_hi = jnp.where(low, 0.0, k2)
            v_hi = jnp.where(low, 0.0, v2)
            k_lo = pltpu.roll(k_hi, HEAD_DIM, 1)
            v_lo = pltpu.roll(v_hi, HEAD_DIM, 1)
        k_halves = (k_lo.astype(BF16), k_hi.astype(BF16))
        v_halves = (v_lo.astype(BF16), v_hi.astype(BF16))

        q_slabs = [q[:, (n * pairs + j) * 2 * HEAD_DIM:(n * pairs + j + 1) * 2 * HEAD_DIM] for j in range(pairs)]
        q4 = jnp.concatenate(q_slabs, axis=0)
        out = None
        for half in range(2):
            s = lax.dot_general(q4, k_halves[half], (((1,), (1,)), ((), ())),
                                preferred_element_type=F32)
            probs = []
            for j in range(pairs):
                sink = sink_ref[n * group + 2 * j + half]
                sj = jnp.where(valid, s[j * t:(j + 1) * t], NEG_INF)
                m = jnp.maximum(jnp.max(sj, axis=-1, keepdims=True), sink)
                pj = jnp.exp(sj - m)
                pj = pj / (jnp.sum(pj, axis=-1, keepdims=True) + jnp.exp(sink - m))
                probs.append(pj.astype(BF16))
            part = _dot(jnp.concatenate(probs, axis=0), v_halves[half])
            out = part if out is None else out + part
        for j in range(pairs):
            col = (n * pairs + j) * 2 * HEAD_DIM
            o_ref[:, col:col + 2 * HEAD_DIM] = out[j * t:(j + 1) * t].astype(o_ref.dtype)


def attn_prompt(z, sinks, *, batch, seq, n_heads, n_kv):
    t = WINDOW
    nb = seq // t
    hq = n_heads * HEAD_DIM
    hk = n_kv * HEAD_DIM
    kcol = hq // hk
    vcol = kcol + 1
    assert math.frexp(HEAD_DIM ** -0.5)[0] == 0.5

    def cur(b, i, s):
        return b * nb + i

    def prev(b, i, s):
        return b * nb + jnp.maximum(i - 1, 0)

    grid_spec = pltpu.PrefetchScalarGridSpec(
        num_scalar_prefetch=1,
        grid=(batch, nb),
        in_specs=[
            pl.BlockSpec((t, hq), lambda b, i, s: (cur(b, i, s), 0)),
            pl.BlockSpec((t, hk), lambda b, i, s: (prev(b, i, s), kcol)),
            pl.BlockSpec((t, hk), lambda b, i, s: (cur(b, i, s), kcol)),
            pl.BlockSpec((t, hk), lambda b, i, s: (prev(b, i, s), vcol)),
            pl.BlockSpec((t, hk), lambda b, i, s: (cur(b, i, s), vcol)),
        ],
        out_specs=pl.BlockSpec((t, hq), lambda b, i, s: (cur(b, i, s), 0)),
    )
    return pl.pallas_call(
        functools.partial(_attn_prompt_body, n_kv=n_kv, group=n_heads // n_kv, scale=HEAD_DIM ** -0.5),
        grid_spec=grid_spec,
        out_shape=jax.ShapeDtypeStruct((batch * seq, hq), BF16),
        compiler_params=_params(2),
        name="attn_prompt",
    )(sinks, z, z, z, z, z)


def _attn_sample_body(q_ref, kc_ref, vc_ref, kn_ref, vn_ref, sink_ref, o_ref, *, n_kv, group, scale, first_valid):
    n_heads, hd = q_ref.shape[1:]
    hk = n_kv * hd
    head = lax.broadcasted_iota(jnp.int32, (n_heads, hk), 0)
    lane = lax.broadcasted_iota(jnp.int32, (n_heads, hk), 1)
    own = (lane // hd) == (head // group)
    sink = sink_ref[...]
    for b in range(q_ref.shape[0]):
        q = q_ref[b]
        q2 = jnp.concatenate([q, q], axis=1)
        q_wide = jnp.where(own, jnp.concatenate([q2] * (n_kv // 2), axis=1), 0.0).astype(BF16)
        s_c = lax.dot_general(q_wide, kc_ref[b].astype(BF16), (((1,), (1,)), ((), ())),
                              preferred_element_type=F32) * scale
        kb = kn_ref[b].astype(BF16).astype(F32)
        s_n = jnp.sum(q_wide.astype(F32) * kb, axis=-1, keepdims=True) * scale
        if first_valid > 0:
            col = lax.broadcasted_iota(jnp.int32, s_c.shape, 1)
            s_c = jnp.where(col >= first_valid, s_c, NEG_INF)
        m = jnp.maximum(jnp.maximum(jnp.max(s_c, axis=-1, keepdims=True), s_n), sink)
        p_c = jnp.exp(s_c - m)
        p_n = jnp.exp(s_n - m)
        denom = jnp.sum(p_c, axis=-1, keepdims=True) + p_n + jnp.exp(sink - m)
        p_c = p_c / denom
        p_n = p_n / denom
        o_wide = _dot(p_c.astype(BF16), vc_ref[b].astype(BF16))
        o_wide = o_wide + p_n.astype(BF16).astype(F32) * vn_ref[b].astype(BF16).astype(F32)
        o_wide = jnp.where(own, o_wide, 0.0)
        o2 = o_wide[:, 0:2 * hd]
        for c in range(1, n_kv // 2):
            o2 = o2 + o_wide[:, c * 2 * hd:(c + 1) * 2 * hd]
        o_ref[b] = o2[:, 0:hd] + o2[:, hd:2 * hd]


def attn_sample(q, kc, vc, kn, vn, sinks, *, n_kv):
    nseq, n_heads, hd = q.shape
    wb = kc.shape[1]
    hk = kc.shape[2]
    first_valid = max(0, wb - WINDOW, wb - PAST_LEN)
    sb = _tile(nseq, SAMPLE_SEQS)
    return pl.pallas_call(
        functools.partial(_attn_sample_body, n_kv=n_kv, group=n_heads // n_kv, scale=HEAD_DIM ** -0.5,
                          first_valid=first_valid),
        grid=(nseq // sb,),
        in_specs=[
            pl.BlockSpec((sb, n_heads, hd), lambda b: (b, 0, 0)),
            pl.BlockSpec((sb, wb, hk), lambda b: (b, 0, 0)),
            pl.BlockSpec((sb, wb, hk), lambda b: (b, 0, 0)),
            pl.BlockSpec((sb, 1, hk), lambda b: (b, 0, 0)),
            pl.BlockSpec((sb, 1, hk), lambda b: (b, 0, 0)),
            pl.BlockSpec((n_heads, 1), lambda b: (0, 0)),
        ],
        out_specs=pl.BlockSpec((sb, n_heads, hd), lambda b: (b, 0, 0)),
        out_shape=jax.ShapeDtypeStruct((nseq, n_heads, hd), F32),
        compiler_params=_params(1),
        name="attn_sample",
    )(q, kc, vc, kn, vn, sinks)


def _conv_mix_prompt_body(gb_ref, cu_ref, halo_ref, w_ref, o_ref, pad_ref, *, tiles_per_seq):
    tm = cu_ref.shape[0]
    first = (pl.program_id(0) % tiles_per_seq) == 0
    halo = jnp.where(first, 0.0, halo_ref[...])
    pad_ref[0:8, :] = halo
    pad_ref[8:8 + tm, :] = cu_ref[...]
    w = w_ref[...]
    dw = w[0:1] * pad_ref[6:6 + tm, :] + w[1:2] * pad_ref[7:7 + tm, :] + w[2:3] * pad_ref[8:8 + tm, :]
    o_ref[...] = (gb_ref[...] * dw).astype(o_ref.dtype)


def conv_mix_prompt(gb, cu, w, *, seq, tm, tn):
    m, d = gb.shape
    hb = tm // 8
    return pl.pallas_call(
        functools.partial(_conv_mix_prompt_body, tiles_per_seq=seq // tm),
        grid=(m // tm, d // tn),
        in_specs=[
            pl.BlockSpec((tm, tn), lambda i, j: (i, j)),
            pl.BlockSpec((tm, tn), lambda i, j: (i, j)),
            pl.BlockSpec((8, tn), lambda i, j: (jnp.maximum(i * hb - 1, 0), j)),
            pl.BlockSpec((CONV_WIDTH, tn), lambda i, j: (0, j)),
        ],
        out_specs=pl.BlockSpec((tm, tn), lambda i, j: (i, j)),
        out_shape=jax.ShapeDtypeStruct((m, d), BF16),
        scratch_shapes=[pltpu.VMEM((tm + 8, tn), F32)],
        compiler_params=_params(2),
        name="conv_mix_prompt",
    )(gb, cu, cu, w)


def _conv_mix_sample_body(gb_ref, cu_ref, s0_ref, s1_ref, w_ref, o_ref):
    w = w_ref[...]
    dw = w[0:1] * s0_ref[...] + w[1:2] * s1_ref[...] + w[2:3] * cu_ref[...]
    o_ref[...] = (gb_ref[...] * dw).astype(o_ref.dtype)


def conv_mix_sample(gb, cu, s0, s1, w):
    return pl.pallas_call(
        _conv_mix_sample_body,
        out_shape=jax.ShapeDtypeStruct(gb.shape, BF16),
        name="conv_mix_sample",
    )(gb, cu, s0, s1, w)


def kernel(x_prompt, x_sample, cache_k, cache_v, state_conv, p_prompt, p_sample, norm_gains, final_norm_gain,
           w_ffn_gate, w_ffn_up, w_ffn_down, w_qkv, b_qkv, attn_sinks, w_o, b_o, w_conv_in, conv_w, w_conv_out,
           w_ple_proj, w_ple_gate):
    batch, seq, d = x_prompt.shape
    nseq, dec_seq, _ = x_sample.shape
    assert dec_seq == 1 and seq % WINDOW == 0 and CONV_WIDTH == conv_w.shape[1]
    depth = norm_gains.shape[0]
    d_ff = w_ffn_gate.shape[-1]
    n_heads = attn_sinks.shape[1]
    qkv_dim = w_qkv.shape[-1]
    n_kv = (qkv_dim // HEAD_DIM - n_heads) // 2
    hq, hk = n_heads * HEAD_DIM, n_kv * HEAD_DIM
    wb = cache_k.shape[2]
    mp, ms = batch * seq, nseq

    f_pad = _cdiv(d_ff, PANEL_FFN) * PANEL_FFN
    tk_d = _tile(f_pad, DOWN_K)
    panel_qkv = _tile(qkv_dim, PANEL_QKV)
    panel_conv = _tile(d, PANEL_CONV)
    panel_ple = _tile(d, PANEL_PLE)
    panel_out = _tile(d, PANEL_OUT)

    def job_gate(i, h):
        return CastJob(w_ffn_gate, (i, h), PANEL_FFN, d)

    def job_up(i, h):
        return CastJob(w_ffn_up, (i, h), PANEL_FFN, d)

    def job_down(i, h):
        return CastJob(w_ffn_down, (i, h), panel_out, f_pad)

    def job_mixer_in(i):
        if i % 2 == 0:
            return CastJob(w_qkv, (i // 2,), panel_qkv, d)
        return CastJob(w_conv_in, (i // 2,), panel_conv, d)

    def job_mixer_out(i):
        if i % 2 == 0:
            return CastJob(w_o, (i // 2,), panel_out, hq)
        return CastJob(w_conv_out, (i // 2,), panel_out, d)

    def job_ple_gate(i):
        return CastJob(w_ple_gate, (i,), panel_ple, d)

    wpp = w_ple_proj.astype(BF16)

    xp = x_prompt.reshape(mp, d)
    xs = x_sample.reshape(ms, d)
    pp = p_prompt.reshape(depth, mp, -1)
    ps = p_sample.reshape(depth, ms, -1)

    wg = cast_panels(job_gate(0, 0))
    wu = cast_panels(job_up(0, 0))

    nkp, nvp, nks, nvs, ncp, ncs = [], [], [], [], [], []
    for i in range(depth):
        g = norm_gains[i][:, None, :]
        last = i == depth - 1
        j = i // 2

        act, act_s, (wd, w_mi, wu_next) = swiglu_up(
            xp, xs, g[0], wg, wu, d_ff=d_ff, jobs=[job_down(i, 0), job_mixer_in(i), job_up(i, 1)])
        xp, xs, (w_mo,) = matmul_residual(act, act_s, wd, xp, xs, scale=0.5, tk=tk_d, k_valid=d_ff,
                                          jobs=[job_mixer_out(i)])

        if i % 2 == 0:
            zp, zs, (wg_next,) = norm_linear(xp, xs, g[1], w_mi, b_qkv[j][None, :], jobs=[job_gate(i, 1)])
            op = attn_prompt(zp, attn_sinks[j], batch=batch, seq=seq, n_heads=n_heads, n_kv=n_kv)
            kn = zs[:, hq:hq + hk]
            vn = zs[:, hq + hk:]
            os_ = attn_sample(zs[:, :hq].reshape(ms, n_heads, HEAD_DIM),
                              cache_k[j].reshape(ms, wb, hk), cache_v[j].reshape(ms, wb, hk),
                              kn[:, None, :], vn[:, None, :], attn_sinks[j][:, None],
                              n_kv=n_kv)
            os_ = os_.reshape(ms, hq).astype(BF16)
            xp, xs, _ = matmul_residual(op, os_, w_mo, xp, xs, b_o[j][None, :], scale=1.0, tk=hq)
            zp3 = zp.reshape(batch, seq, qkv_dim)
            wp_ = min(WINDOW, seq)
            nkp.append(zp3[:, seq - wp_:, hq:hq + hk].reshape(batch, wp_, n_kv, HEAD_DIM))
            nvp.append(zp3[:, seq - wp_:, hq + hk:].reshape(batch, wp_, n_kv, HEAD_DIM))
            kk = jnp.concatenate([cache_k[j], kn.reshape(ms, 1, n_kv, HEAD_DIM)], axis=1)
            vv = jnp.concatenate([cache_v[j], vn.reshape(ms, 1, n_kv, HEAD_DIM)], axis=1)
            nks.append(kk[:, -wb:])
            nvs.append(vv[:, -wb:])
        else:
            (gbp, cup), (gbs, cus), (wg_next,) = conv_gates(xp, xs, g[1], w_mi, jobs=[job_gate(i, 1)])
            tp = conv_mix_prompt(gbp, cup, conv_w[j], seq=seq, tm=_tile(seq, 512), tn=_tile(d, 1024))
            st = state_conv[j]
            ts = conv_mix_sample(gbs, cus, st[:, 0], st[:, 1], conv_w[j])
            xp, xs, _ = matmul_residual(tp, ts, w_mo, xp, xs, scale=1.0, tk=d)
            ncp.append(cup.reshape(batch, seq, d)[:, seq - (CONV_WIDTH - 1):])
            ncs.append(jnp.concatenate([st, cus[:, None, :]], axis=1)[:, -(CONV_WIDTH - 1):])

        wg, wu = wg_next, wu_next
        act, act_s, (wd, w_pg) = swiglu_up(xp, xs, g[2], wg, wu, d_ff=d_ff, jobs=[job_down(i, 1), job_ple_gate(i)])
        xp, xs, copies = matmul_residual(act, act_s, wd, xp, xs, scale=0.5, tk=tk_d, k_valid=d_ff,
                                         jobs=[] if last else [job_gate(i + 1, 0)])
        if not last:
            wg = copies[0]

        xp, xs, copies = ple(xp, xs, g[3], w_pg, pp[i], ps[i], wpp[i], jobs=[] if last else [job_up(i + 1, 0)])
        if not last:
            wu = copies[0]

    gf = final_norm_gain[None, :]
    y_prompt = rmsnorm(xp, gf, tm=_tile(mp, 256)).reshape(batch, seq, d)
    y_sample = rmsnorm(xs, gf, tm=ms).reshape(nseq, dec_seq, d)
    return (y_prompt, y_sample, jnp.stack(nkp), jnp.stack(nvp), jnp.stack(nks), jnp.stack(nvs),
            jnp.stack(ncp), jnp.stack(ncs))
```

```python
import functools
import math
from typing import NamedTuple

import jax
import jax.numpy as jnp
from jax import lax
from jax.experimental import pallas as pl
from jax.experimental.pallas import tpu as pltpu

NORM_EPS = 1e-6
NEG_INF = -1e30
HEAD_DIM = 64
WINDOW = 128
CONV_WIDTH = 3
PAST_LEN = 8192

V7X_VMEM_LIMIT_BYTES = 58 * 1024 * 1024
BF16_SUBLANES = 16

GROUP_ROWS = 1024
ROW_TILE = 512
NORM_ROWS = 256
PANEL_FFN = 512
PANEL_QKV = 512
PANEL_CONV = 256
PANEL_PLE = 512
PANEL_OUT = 1024
DOWN_ROWS = 1024
DOWN_K = 2816
SAMPLE_SEQS = 8

BF16 = jnp.bfloat16
F32 = jnp.float32


def _params(n_axes):
    return pltpu.CompilerParams(
        dimension_semantics=("arbitrary",) * n_axes,
        vmem_limit_bytes=V7X_VMEM_LIMIT_BYTES,
    )


def _tile(dim, target):
    if dim <= target:
        return dim
    t = target
    while dim % t:
        t //= 2
    return t


def _cdiv(a, b):
    return -(-a // b)


def _dot(a, b):
    return jnp.dot(a, b, preferred_element_type=F32)


class CastJob(NamedTuple):
    src: jax.Array
    lead: tuple
    tn: int
    rows_out: int


class _JobPlan(NamedTuple):
    rows: int
    cols: int
    tn: int
    panels: int
    rb: int
    nb_src: int
    nb_dst: int
    rows_out: int


def _plan_job(job, n_steps):
    rows, cols = job.src.shape[-2:]
    rb = BF16_SUBLANES
    while _cdiv(job.rows_out, rb) > n_steps:
        rb *= 2
    assert rows % rb == 0 and job.rows_out % rb == 0, (rows, job.rows_out, rb)
    return _JobPlan(rows, cols, job.tn, _cdiv(cols, job.tn), rb, rows // rb, job.rows_out // rb, job.rows_out)


def _job_specs(job, plan, step_of):
    n_lead = len(job.lead)

    def src_idx(*g):
        return (*job.lead, jnp.minimum(step_of(*g), plan.nb_src - 1), 0)

    def dst_idx(*g):
        return (0, jnp.minimum(step_of(*g), plan.nb_dst - 1), 0)

    src_spec = pl.BlockSpec((None,) * n_lead + (plan.rb, plan.cols), src_idx)
    dst_spec = pl.BlockSpec((plan.panels, plan.rb, plan.tn), dst_idx)
    dst_shape = jax.ShapeDtypeStruct((plan.panels, plan.rows_out, plan.tn), BF16)
    return src_spec, dst_spec, dst_shape


def _run_job(plan, src_ref, dst_ref, step):
    v = src_ref[...]
    if plan.rows_out > plan.rows:
        blk = jnp.minimum(step, plan.nb_dst - 1)
        row = blk * plan.rb + lax.broadcasted_iota(jnp.int32, (plan.rb, 1), 0)
        v = jnp.where(row < plan.rows, v, 0.0)
    for p in range(plan.panels):
        lo = p * plan.tn
        width = min(plan.tn, plan.cols - lo)
        dst_ref[p, :, 0:width] = v[:, lo:lo + width].astype(BF16)
        if width < plan.tn:
            dst_ref[p, :, width:plan.tn] = jnp.zeros((plan.rb, plan.tn - width), BF16)


def _call_with_jobs(body, *, name, grid, in_specs, out_specs, out_shape, scratch_shapes, args, jobs):
    n_steps = 1
    for gdim in grid:
        n_steps *= gdim

    def step_of(*g):
        s = g[0]
        for gdim, gi in zip(grid[1:], g[1:]):
            s = s * gdim + gi
        return s

    plans = [_plan_job(j, n_steps) for j in jobs]
    specs = [_job_specs(j, p, step_of) for j, p in zip(jobs, plans)]
    n_in, n_out, n_job = len(in_specs), len(out_specs), len(jobs)

    def kern(*refs):
        ins = refs[:n_in]
        job_src = refs[n_in:n_in + n_job]
        outs = refs[n_in + n_job:n_in + n_job + n_out]
        job_dst = refs[n_in + n_job + n_out:n_in + 2 * n_job + n_out]
        scratch = refs[n_in + 2 * n_job + n_out:]

        def run_jobs():
            if n_job:
                step = step_of(*[pl.program_id(a) for a in range(len(grid))])
                for plan, s_ref, d_ref in zip(plans, job_src, job_dst):
                    _run_job(plan, s_ref, d_ref, step)

        body(run_jobs, *ins, *outs, *scratch)

    res = pl.pallas_call(
        kern,
        grid=grid,
        in_specs=list(in_specs) + [s[0] for s in specs],
        out_specs=list(out_specs) + [s[1] for s in specs],
        out_shape=list(out_shape) + [s[2] for s in specs],
        scratch_shapes=scratch_shapes,
        compiler_params=_params(len(grid)),
        name=name,
    )(*args, *[j.src for j in jobs])
    return list(res[:n_out]), list(res[n_out:])


def cast_panels(job):
    n_steps = _cdiv(job.rows_out, 2 * BF16_SUBLANES)
    _, copies = _call_with_jobs(lambda run_jobs: run_jobs(), name="cast_panels", grid=(n_steps,), in_specs=[],
                                out_specs=[], out_shape=[], scratch_shapes=[], args=[], jobs=[job])
    return copies[0]


def _norm_rows_into(h_ref, row0, x_ref, g_ref):
    g = g_ref[...]
    chunk = BF16_SUBLANES

    def body(c, carry):
        r = pl.multiple_of(c * chunk, chunk)
        x = x_ref[pl.ds(r, chunk), :]
        y = x * lax.rsqrt(jnp.mean(x * x, axis=-1, keepdims=True) + NORM_EPS) * g
        h_ref[pl.ds(row0 + r, chunk), :] = y.astype(BF16)
        return carry

    n_chunks = x_ref.shape[0] // chunk
    lax.fori_loop(0, n_chunks, body, 0, unroll=min(n_chunks, 8))


class _Extra(NamedTuple):
    array: jax.Array
    cols: int
    by_panel: bool
    by_row: bool
    sample: object


def _norm_matmul(name, epilogue, x, g, panels, extras, outs, *, tn, n_panels, x_sample, last_cols=None, jobs=()):
    m, d = x.shape
    last_cols = tn if last_cols is None else last_cols
    assert last_cols == tn or not any(e.by_panel for e in extras)
    ms = x_sample.shape[0]
    group = _tile(m, GROUP_ROWS)
    tm = _tile(group, ROW_TILE)
    xb = _tile(group, NORM_ROWS)
    pre = group // xb
    n_groups = m // group
    last = n_groups - 1
    grid = (n_groups, pre + n_panels)
    n_p, n_e, n_o = len(panels), len(extras), len(outs)
    row_extras = [k for k, e in enumerate(extras) if e.by_row]

    def panel_of(n):
        return jnp.maximum(n - pre, 0)

    def sample_panel_of(i, n):
        return jnp.where(i == last, panel_of(n), 0)

    in_specs = [
        pl.BlockSpec((xb, d), lambda i, n: (i * pre + jnp.minimum(n, pre - 1), 0)),
        pl.BlockSpec((1, d), lambda i, n: (0, 0)),
    ]
    args = [x, g]
    for w, off in panels:
        in_specs.append(pl.BlockSpec((None, d, tn), functools.partial(lambda i, n, o: (panel_of(n) + o, 0, 0), o=off)))
        args.append(w)
    for e in extras:
        rows = group if e.by_row else e.array.shape[0]
        in_specs.append(pl.BlockSpec(
            (rows, e.cols),
            functools.partial(lambda i, n, r, c: (i if r else 0, panel_of(n) if c else 0), r=e.by_row, c=e.by_panel)))
        args.append(e.array)
    in_specs.append(pl.BlockSpec((ms, d), lambda i, n: (0, 0)))
    args.append(x_sample)
    for k in row_extras:
        e = extras[k]
        in_specs.append(pl.BlockSpec(
            (ms, e.cols), functools.partial(lambda i, n, c: (0, sample_panel_of(i, n) if c else 0), c=e.by_panel)))
        args.append(e.sample)
    n_in = len(in_specs)

    out_specs = [pl.BlockSpec((group, tn), lambda i, n: (i, panel_of(n))) for _ in outs]
    out_specs += [pl.BlockSpec((ms, tn), lambda i, n: (0, sample_panel_of(i, n))) for _ in outs]
    out_shape = [jax.ShapeDtypeStruct((m, cols), dt) for cols, dt in outs]
    out_shape += [jax.ShapeDtypeStruct((ms, cols), dt) for cols, dt in outs]

    def body(run_jobs, *refs):
        x_ref, g_ref = refs[:2]
        panel_refs = refs[2:2 + n_p]
        extra_refs = refs[2 + n_p:2 + n_p + n_e]
        xs_ref = refs[2 + n_p + n_e]
        extra_s_refs = dict(zip(row_extras, refs[3 + n_p + n_e:n_in]))
        out_refs = refs[n_in:n_in + n_o]
        out_s_refs = refs[n_in + n_o:n_in + 2 * n_o]
        h_ref = refs[n_in + 2 * n_o]
        i, n = pl.program_id(0), pl.program_id(1)

        @pl.when(n < pre)
        def _():
            _norm_rows_into(h_ref, pl.multiple_of(n * xb, xb), x_ref, g_ref)
            run_jobs()

        @pl.when((n == 0) & (i == last))
        def _():
            _norm_rows_into(h_ref, group, xs_ref, g_ref)

        def store(ref, rows, v):
            width = v.shape[1]
            ref[rows, 0:width] = v
            if width < tn:
                ref[rows, width:tn] = jnp.zeros((v.shape[0], tn - width), v.dtype)

        def panel_getter(cols):
            return lambda k: panel_refs[k][...] if cols == tn else panel_refs[k][:, 0:cols]

        def panel_step(cols, with_sample):
            n_sub = group // tm
            for s in range(n_sub):
                rows = slice(s * tm, (s + 1) * tm)
                joined = with_sample and s == n_sub - 1

                def get_extra(k, rows=rows, joined=joined):
                    if not extras[k].by_row:
                        return extra_refs[k][...]
                    v = extra_refs[k][rows, :]
                    return jnp.concatenate([v, extra_s_refs[k][...]], axis=0) if joined else v

                def put_out(k, v, rows=rows, joined=joined):
                    store(out_refs[k], rows, v[0:tm])
                    if joined:
                        store(out_s_refs[k], slice(None), v[tm:tm + ms])

                h_rows = slice(s * tm, group + ms) if joined else rows
                epilogue(h_ref[h_rows, :], panel_getter(cols), get_extra, put_out)
            run_jobs()

        last_panel = pre + n_panels - 1
        variants = [(n >= pre, tn)] if last_cols == tn else [((n >= pre) & (n < last_panel), tn),
                                                            (n == last_panel, last_cols)]
        for cond, cols in variants:
            pl.when(cond & (i < last))(functools.partial(panel_step, cols, False))
            pl.when(cond & (i == last))(functools.partial(panel_step, cols, True))

    outs_all, copies = _call_with_jobs(
        body, name=name, grid=grid, in_specs=in_specs, out_specs=out_specs, out_shape=out_shape,
        scratch_shapes=[pltpu.VMEM((group + ms, d), BF16)], args=args, jobs=list(jobs))
    return outs_all[:n_o], outs_all[n_o:], copies


def _swiglu_epilogue(h, panel, get_extra, put_out):
    gate = _dot(h, panel(0))
    up = _dot(h, panel(1))
    put_out(0, (gate * jax.nn.sigmoid(gate) * up).astype(BF16))


def swiglu_up(x, x_sample, g, wg, wu, *, d_ff, jobs=()):
    n_panels, _, tn = wg.shape
    outs, outs_s, copies = _norm_matmul(
        "swiglu_up", _swiglu_epilogue, x, g, [(wg, 0), (wu, 0)], [], [(n_panels * tn, BF16)],
        tn=tn, n_panels=n_panels, x_sample=x_sample, last_cols=d_ff - (n_panels - 1) * tn, jobs=jobs)
    return outs[0], outs_s[0], copies


def _linear_epilogue(h, panel, get_extra, put_out):
    put_out(0, _dot(h, panel(0)) + get_extra(0))


def norm_linear(x, x_sample, g, w, b, *, jobs=()):
    n_panels, _, tn = w.shape
    outs, outs_s, copies = _norm_matmul(
        "norm_linear", _linear_epilogue, x, g, [(w, 0)], [_Extra(b, tn, True, False, None)],
        [(n_panels * tn, F32)], tn=tn, n_panels=n_panels, x_sample=x_sample, jobs=jobs)
    return outs[0], outs_s[0], copies


def _conv_gates_epilogue(h, panel, get_extra, put_out):
    put_out(0, _dot(h, panel(0)))
    put_out(1, _dot(h, panel(1)) * _dot(h, panel(2)))


def conv_gates(x, x_sample, g, w_in, *, jobs=()):
    d = x.shape[1]
    tn = w_in.shape[2]
    n_panels = d // tn
    outs, outs_s, copies = _norm_matmul(
        "conv_gates", _conv_gates_epilogue, x, g, [(w_in, 0), (w_in, n_panels), (w_in, 2 * n_panels)], [],
        [(d, F32), (d, F32)], tn=tn, n_panels=n_panels, x_sample=x_sample, jobs=jobs)
    return outs, outs_s, copies


def _ple_epilogue(h, panel, get_extra, put_out):
    gate = jax.nn.sigmoid(_dot(h, panel(0)))
    proj = _dot(get_extra(1).astype(BF16), get_extra(2))
    put_out(0, get_extra(0) + gate * proj)


def ple(x, x_sample, g, w_gate, p, p_sample, w_proj, *, jobs=()):
    d = x.shape[1]
    pd = p.shape[1]
    n_panels, _, tn = w_gate.shape
    extras = [
        _Extra(x, tn, True, True, x_sample),
        _Extra(p, pd, False, True, p_sample),
        _Extra(w_proj, tn, True, False, None),
    ]
    outs, outs_s, copies = _norm_matmul(
        "ple", _ple_epilogue, x, g, [(w_gate, 0)], extras, [(d, F32)],
        tn=tn, n_panels=n_panels, x_sample=x_sample, jobs=jobs)
    return outs[0], outs_s[0], copies


class _RowSet(NamedTuple):
    a: object
    x: object
    o: object
    acc: object


def _residual_step(row_sets, w_ref, b_ref, *, scale, n_k, k, last_k_cols, then):
    tk, tn = w_ref.shape
    halves = [slice(c * (tn // 2), (c + 1) * (tn // 2)) for c in range(2)] if tn % 256 == 0 else [slice(0, tn)]

    def products(cols, k_cols):
        parts = [rs.a[...] if k_cols == tk else rs.a[:, 0:k_cols] for rs in row_sets]
        a = parts[0] if len(parts) == 1 else jnp.concatenate(parts, axis=0)
        y = _dot(a, w_ref[:, cols] if k_cols == tk else w_ref[0:k_cols, cols])
        out, r0 = [], 0
        for p in parts:
            out.append(y[r0:r0 + p.shape[0]])
            r0 += p.shape[0]
        return out

    def finish(rs, y, cols):
        if b_ref is not None:
            y = y + b_ref[:, cols]
        rs.o[:, cols] = rs.x[:, cols] + scale * y

    if n_k == 1:
        for cols in halves:
            for rs, y in zip(row_sets, products(cols, last_k_cols)):
                finish(rs, y, cols)
        then()
        return

    @pl.when(k < n_k - 1)
    def _():
        for cols in halves:
            for rs, y in zip(row_sets, products(cols, tk)):
                rs.acc[:, cols] += y
        then()

    @pl.when(k == n_k - 1)
    def _():
        for cols in halves:
            for rs, y in zip(row_sets, products(cols, last_k_cols)):
                acc = rs.acc[:, cols] + y
                rs.acc[:, cols] = jnp.zeros_like(acc)
                finish(rs, acc, cols)
        then()


def _matmul_residual_body(run_jobs, *refs, scale, has_bias, n_k, last_tile, last_k_cols):
    a_ref, w_ref, x_ref, as_ref, xs_ref = refs[:5]
    b_ref = refs[5] if has_bias else None
    o_ref, os_ref = refs[5 + has_bias:7 + has_bias]
    acc_ref, accs_ref = refs[7 + has_bias:9 + has_bias] if n_k > 1 else (None, None)
    i, j, k = pl.program_id(0), pl.program_id(1), pl.program_id(2)
    prompt = _RowSet(a_ref, x_ref, o_ref, acc_ref)
    sample = _RowSet(as_ref, xs_ref, os_ref, accs_ref)

    if n_k > 1:
        @pl.when((i == 0) & (j == 0) & (k == 0))
        def _():
            acc_ref[...] = jnp.zeros_like(acc_ref)

        @pl.when((i == last_tile) & (j == 0) & (k == 0))
        def _():
            accs_ref[...] = jnp.zeros_like(accs_ref)

    step = functools.partial(_residual_step, w_ref=w_ref, b_ref=b_ref, scale=scale, n_k=n_k, k=k,
                             last_k_cols=last_k_cols, then=run_jobs)
    pl.when(i < last_tile)(lambda: step([prompt]))
    pl.when(i == last_tile)(lambda: step([prompt, sample]))


def matmul_residual(a, a_sample, w, x, x_sample, b=None, *, scale, tk, k_valid=None, jobs=()):
    m, kdim = a.shape
    ms = a_sample.shape[0]
    n_panels, kw, tn = w.shape
    tm = _tile(m, DOWN_ROWS)
    assert kw == kdim and kdim % tk == 0 and m % tm == 0
    n_k = kdim // tk
    last_tile = m // tm - 1
    last_k_cols = tk if k_valid is None else k_valid - (n_k - 1) * tk
    assert 0 < last_k_cols <= tk and last_k_cols % 128 == 0

    def sample_col(i, j):
        return jnp.where(i == last_tile, j, 0)

    in_specs = [
        pl.BlockSpec((tm, tk), lambda i, j, k: (i, k)),
        pl.BlockSpec((None, tk, tn), lambda i, j, k: (j, k, 0)),
        pl.BlockSpec((tm, tn), lambda i, j, k: (i, j)),
        pl.BlockSpec((ms, tk), lambda i, j, k: (0, k)),
        pl.BlockSpec((ms, tn), lambda i, j, k: (0, sample_col(i, j))),
    ]
    args = [a, w, x, a_sample, x_sample]
    if b is not None:
        in_specs.append(pl.BlockSpec((1, tn), lambda i, j, k: (0, j)))
        args.append(b)
    scratch = [pltpu.VMEM((tm, tn), F32), pltpu.VMEM((ms, tn), F32)] if n_k > 1 else []
    outs, copies = _call_with_jobs(
        functools.partial(_matmul_residual_body, scale=scale, has_bias=b is not None, n_k=n_k, last_tile=last_tile,
                          last_k_cols=last_k_cols),
        name="matmul_residual", grid=(m // tm, n_panels, n_k), in_specs=in_specs,
        out_specs=[pl.BlockSpec((tm, tn), lambda i, j, k: (i, j)),
                   pl.BlockSpec((ms, tn), lambda i, j, k: (0, sample_col(i, j)))],
        out_shape=[jax.ShapeDtypeStruct((m, n_panels * tn), F32), jax.ShapeDtypeStruct((ms, n_panels * tn), F32)],
        scratch_shapes=scratch, args=args, jobs=list(jobs))
    return outs[0], outs[1], copies


def _rmsnorm_kernel(x_ref, g_ref, o_ref):
    x = x_ref[...]
    o_ref[...] = x * lax.rsqrt(jnp.mean(x * x, axis=-1, keepdims=True) + NORM_EPS) * g_ref[...]


def rmsnorm(x, g, *, tm):
    m, d = x.shape
    return pl.pallas_call(
        _rmsnorm_kernel,
        grid=(m // tm,),
        in_specs=[pl.BlockSpec((tm, d), lambda i: (i, 0)), pl.BlockSpec((1, d), lambda i: (0, 0))],
        out_specs=pl.BlockSpec((tm, d), lambda i: (i, 0)),
        out_shape=jax.ShapeDtypeStruct((m, d), F32),
        compiler_params=_params(1),
        name="rmsnorm",
    )(x, g)


def _attn_prompt_body(sink_ref, q_ref, kp_ref, kc_ref, vp_ref, vc_ref, o_ref, *, n_kv, group, scale):
    blk = pl.program_id(1)
    t = WINDOW
    pairs = group // 2
    q = (q_ref[...] * scale).astype(BF16)
    k_all = jnp.concatenate([kp_ref[...], kc_ref[...]], axis=0)
    v_all = jnp.concatenate([vp_ref[...], vc_ref[...]], axis=0)

    r = lax.broadcasted_iota(jnp.int32, (t, 2 * t), 0)
    c = lax.broadcasted_iota(jnp.int32, (t, 2 * t), 1)
    first_col = jnp.where(blk > 0, r, t)
    valid = (c >= first_col) & (c <= r + t)
    lane = lax.broadcasted_iota(jnp.int32, (2 * t, 2 * HEAD_DIM), 1)
    low = lane < HEAD_DIM

    for n in range(n_kv):
        slab = slice((n // 2) * 2 * HEAD_DIM, (n // 2 + 1) * 2 * HEAD_DIM)
        k2 = k_all[:, slab]
        v2 = v_all[:, slab]
        if n % 2 == 0:
            k_lo = jnp.where(low, k2, 0.0)
            v_lo = jnp.where(low, v2, 0.0)
            k_hi = pltpu.roll(k_lo, HEAD_DIM, 1)
            v_hi = pltpu.roll(v_lo, HEAD_DIM, 1)
        else:
            k_hi = jnp.where(low, 0.0, k2)
            v_hi = jnp.where(low, 0.0, v2)
            k_lo = pltpu.roll(k_hi, HEAD_DIM, 1)
            v_lo = pltpu.roll(v_hi, HEAD_DIM, 1)
        k_halves = (k_lo.astype(BF16), k_hi.astype(BF16))
        v_halves = (v_lo.astype(BF16), v_hi.astype(BF16))

        q_slabs = [q[:, (n * pairs + j) * 2 * HEAD_DIM:(n * pairs + j + 1) * 2 * HEAD_DIM] for j in range(pairs)]
        q4 = jnp.concatenate(q_slabs, axis=0)
        out = None
        for half in range(2):
            s = lax.dot_general(q4, k_halves[half], (((1,), (1,)), ((), ())),
                                preferred_element_type=F32)
            probs = []
            for j in range(pairs):
                sink = sink_ref[n * group + 2 * j + half]
                sj = jnp.where(valid, s[j * t:(j + 1) * t], NEG_INF)
                m = jnp.maximum(jnp.max(sj, axis=-1, keepdims=True), sink)
                pj = jnp.exp(sj - m)
                pj = pj / (jnp.sum(pj, axis=-1, keepdims=True) + jnp.exp(sink - m))
                probs.append(pj.astype(BF16))
            part = _dot(jnp.concatenate(probs, axis=0), v_halves[half])
            out = part if out is None else out + part
        for j in range(pairs):
            col = (n * pairs + j) * 2 * HEAD_DIM
            o_ref[:, col:col + 2 * HEAD_DIM] = out[j * t:(j + 1) * t].astype(o_ref.dtype)


def attn_prompt(z, sinks, *, batch, seq, n_heads, n_kv):
    t = WINDOW
    nb = seq // t
    hq = n_heads * HEAD_DIM
    hk = n_kv * HEAD_DIM
    kcol = hq // hk
    vcol = kcol + 1
    assert math.frexp(HEAD_DIM ** -0.5)[0] == 0.5

    def cur(b, i, s):
        return b * nb + i

    def prev(b, i, s):
        return b * nb + jnp.maximum(i - 1, 0)

    grid_spec = pltpu.PrefetchScalarGridSpec(
        num_scalar_prefetch=1,
        grid=(batch, nb),
        in_specs=[
            pl.BlockSpec((t, hq), lambda b, i, s: (cur(b, i, s), 0)),
            pl.BlockSpec((t, hk), lambda b, i, s: (prev(b, i, s), kcol)),
            pl.BlockSpec((t, hk), lambda b, i, s: (cur(b, i, s), kcol)),
            pl.BlockSpec((t, hk), lambda b, i, s: (prev(b, i, s), vcol)),
            pl.BlockSpec((t, hk), lambda b, i, s: (cur(b, i, s), vcol)),
        ],
        out_specs=pl.BlockSpec((t, hq), lambda b, i, s: (cur(b, i, s), 0)),
    )
    return pl.pallas_call(
        functools.partial(_attn_prompt_body, n_kv=n_kv, group=n_heads // n_kv, scale=HEAD_DIM ** -0.5),
        grid_spec=grid_spec,
        out_shape=jax.ShapeDtypeStruct((batch * seq, hq), BF16),
        compiler_params=_params(2),
        name="attn_prompt",
    )(sinks, z, z, z, z, z)


def _attn_sample_body(q_ref, kc_ref, vc_ref, kn_ref, vn_ref, sink_ref, o_ref, *, n_kv, group, scale, first_valid):
    n_heads, hd = q_ref.shape[1:]
    hk = n_kv * hd
    head = lax.broadcasted_iota(jnp.int32, (n_heads, hk), 0)
    lane = lax.broadcasted_iota(jnp.int32, (n_heads, hk), 1)
    own = (lane // hd) == (head // group)
    sink = sink_ref[...]
    for b in range(q_ref.shape[0]):
        q = q_ref[b]
        q2 = jnp.concatenate([q, q], axis=1)
        q_wide = jnp.where(own, jnp.concatenate([q2] * (n_kv // 2), axis=1), 0.0).astype(BF16)
        s_c = lax.dot_general(q_wide, kc_ref[b].astype(BF16), (((1,), (1,)), ((), ())),
                              preferred_element_type=F32) * scale
        kb = kn_ref[b].astype(BF16).astype(F32)
        s_n = jnp.sum(q_wide.astype(F32) * kb, axis=-1, keepdims=True) * scale
        if first_valid > 0:
            col = lax.broadcasted_iota(jnp.int32, s_c.shape, 1)
            s_c = jnp.where(col >= first_valid, s_c, NEG_INF)
        m = jnp.maximum(jnp.maximum(jnp.max(s_c, axis=-1, keepdims=True), s_n), sink)
        p_c = jnp.exp(s_c - m)
        p_n = jnp.exp(s_n - m)
        denom = jnp.sum(p_c, axis=-1, keepdims=True) + p_n + jnp.exp(sink - m)
        p_c = p_c / denom
        p_n = p_n / denom
        o_wide = _dot(p_c.astype(BF16), vc_ref[b].astype(BF16))
        o_wide = o_wide + p_n.astype(BF16).astype(F32) * vn_ref[b].astype(BF16).astype(F32)
        o_wide = jnp.where(own, o_wide, 0.0)
        o2 = o_wide[:, 0:2 * hd]
        for c in range(1, n_kv // 2):
            o2 = o2 + o_wide[:, c * 2 * hd:(c + 1) * 2 * hd]
        o_ref[b] = o2[:, 0:hd] + o2[:, hd:2 * hd]


def attn_sample(q, kc, vc, kn, vn, sinks, *, n_kv):
    nseq, n_heads, hd = q.shape
    wb = kc.shape[1]
    hk = kc.shape[2]
    first_valid = max(0, wb - WINDOW, wb - PAST_LEN)
    sb = _tile(nseq, SAMPLE_SEQS)
    return pl.pallas_call(
        functools.partial(_attn_sample_body, n_kv=n_kv, group=n_heads // n_kv, scale=HEAD_DIM ** -0.5,
                          first_valid=first_valid),
        grid=(nseq // sb,),
        in_specs=[
            pl.BlockSpec((sb, n_heads, hd), lambda b: (b, 0, 0)),
            pl.BlockSpec((sb, wb, hk), lambda b: (b, 0, 0)),
            pl.BlockSpec((sb, wb, hk), lambda b: (b, 0, 0)),
            pl.BlockSpec((sb, 1, hk), lambda b: (b, 0, 0)),
            pl.BlockSpec((sb, 1, hk), lambda b: (b, 0, 0)),
            pl.BlockSpec((n_heads, 1), lambda b: (0, 0)),
        ],
        out_specs=pl.BlockSpec((sb, n_heads, hd), lambda b: (b, 0, 0)),
        out_shape=jax.ShapeDtypeStruct((nseq, n_heads, hd), F32),
        compiler_params=_params(1),
        name="attn_sample",
    )(q, kc, vc, kn, vn, sinks)


def _conv_mix_prompt_body(gb_ref, cu_ref, halo_ref, w_ref, o_ref, pad_ref, *, tiles_per_seq):
    tm = cu_ref.shape[0]
    first = (pl.program_id(0) % tiles_per_seq) == 0
    halo = jnp.where(first, 0.0, halo_ref[...])
    pad_ref[0:8, :] = halo
    pad_ref[8:8 + tm, :] = cu_ref[...]
    w = w_ref[...]
    dw = w[0:1] * pad_ref[6:6 + tm, :] + w[1:2] * pad_ref[7:7 + tm, :] + w[2:3] * pad_ref[8:8 + tm, :]
    o_ref[...] = (gb_ref[...] * dw).astype(o_ref.dtype)


def conv_mix_prompt(gb, cu, w, *, seq, tm, tn):
    m, d = gb.shape
    hb = tm // 8
    return pl.pallas_call(
        functools.partial(_conv_mix_prompt_body, tiles_per_seq=seq // tm),
        grid=(m // tm, d // tn),
        in_specs=[
            pl.BlockSpec((tm, tn), lambda i, j: (i, j)),
            pl.BlockSpec((tm, tn), lambda i, j: (i, j)),
            pl.BlockSpec((8, tn), lambda i, j: (jnp.maximum(i * hb - 1, 0), j)),
            pl.BlockSpec((CONV_WIDTH, tn), lambda i, j: (0, j)),
        ],
        out_specs=pl.BlockSpec((tm, tn), lambda i, j: (i, j)),
        out_shape=jax.ShapeDtypeStruct((m, d), BF16),
        scratch_shapes=[pltpu.VMEM((tm + 8, tn), F32)],
        compiler_params=_params(2),
        name="conv_mix_prompt",
    )(gb, cu, cu, w)


def _conv_mix_sample_body(gb_ref, cu_ref, s0_ref, s1_ref, w_ref, o_ref):
    w = w_ref[...]
    dw = w[0:1] * s0_ref[...] + w[1:2] * s1_ref[...] + w[2:3] * cu_ref[...]
    o_ref[...] = (gb_ref[...] * dw).astype(o_ref.dtype)


def conv_mix_sample(gb, cu, s0, s1, w):
    return pl.pallas_call(
        _conv_mix_sample_body,
        out_shape=jax.ShapeDtypeStruct(gb.shape, BF16),
        name="conv_mix_sample",
    )(gb, cu, s0, s1, w)


def kernel(x_prompt, x_sample, cache_k, cache_v, state_conv, p_prompt, p_sample, norm_gains, final_norm_gain,
           w_ffn_gate, w_ffn_up, w_ffn_down, w_qkv, b_qkv, attn_sinks, w_o, b_o, w_conv_in, conv_w, w_conv_out,
           w_ple_proj, w_ple_gate):
    batch, seq, d = x_prompt.shape
    nseq, dec_seq, _ = x_sample.shape
    assert dec_seq == 1 and seq % WINDOW == 0 and CONV_WIDTH == conv_w.shape[1]
    depth = norm_gains.shape[0]
    d_ff = w_ffn_gate.shape[-1]
    n_heads = attn_sinks.shape[1]
    qkv_dim = w_qkv.shape[-1]
    n_kv = (qkv_dim // HEAD_DIM - n_heads) // 2
    hq, hk = n_heads * HEAD_DIM, n_kv * HEAD_DIM
    wb = cache_k.shape[2]
    mp, ms = batch * seq, nseq

    f_pad = _cdiv(d_ff, PANEL_FFN) * PANEL_FFN
    tk_d = _tile(f_pad, DOWN_K)
    panel_qkv = _tile(qkv_dim, PANEL_QKV)
    panel_conv = _tile(d, PANEL_CONV)
    panel_ple = _tile(d, PANEL_PLE)
    panel_out = _tile(d, PANEL_OUT)

    def job_gate(i, h):
        return CastJob(w_ffn_gate, (i, h), PANEL_FFN, d)

    def job_up(i, h):
        return CastJob(w_ffn_up, (i, h), PANEL_FFN, d)

    def job_down(i, h):
        return CastJob(w_ffn_down, (i, h), panel_out, f_pad)

    def job_mixer_in(i):
        if i % 2 == 0:
            return CastJob(w_qkv, (i // 2,), panel_qkv, d)
        return CastJob(w_conv_in, (i // 2,), panel_conv, d)

    def job_mixer_out(i):
        if i % 2 == 0:
            return CastJob(w_o, (i // 2,), panel_out, hq)
        return CastJob(w_conv_out, (i // 2,), panel_out, d)

    def job_ple_gate(i):
        return CastJob(w_ple_gate, (i,), panel_ple, d)

    wpp = w_ple_proj.astype(BF16)

    xp = x_prompt.reshape(mp, d)
    xs = x_sample.reshape(ms, d)
    pp = p_prompt.reshape(depth, mp, -1)
    ps = p_sample.reshape(depth, ms, -1)

    wg = cast_panels(job_gate(0, 0))
    wu = cast_panels(job_up(0, 0))

    nkp, nvp, nks, nvs, ncp, ncs = [], [], [], [], [], []
    for i in range(depth):
        g = norm_gains[i][:, None, :]
        last = i == depth - 1
        j = i // 2

        act, act_s, (wd, w_mi, wu_next, wg_next) = swiglu_up(
            xp, xs, g[0], wg, wu, d_ff=d_ff,
            jobs=[job_down(i, 0), job_mixer_in(i), job_up(i, 1), job_gate(i, 1)])
        xp, xs, (w_mo,) = matmul_residual(act, act_s, wd, xp, xs, scale=0.5, tk=tk_d, k_valid=d_ff,
                                          jobs=[job_mixer_out(i)])

        if i % 2 == 0:
            zp, zs, _ = norm_linear(xp, xs, g[1], w_mi, b_qkv[j][None, :])
            op = attn_prompt(zp, attn_sinks[j], batch=batch, seq=seq, n_heads=n_heads, n_kv=n_kv)
            kn = zs[:, hq:hq + hk]
            vn = zs[:, hq + hk:]
            os_ = attn_sample(zs[:, :hq].reshape(ms, n_heads, HEAD_DIM),
                              cache_k[j].reshape(ms, wb, hk), cache_v[j].reshape(ms, wb, hk),
                              kn[:, None, :], vn[:, None, :], attn_sinks[j][:, None],
                              n_kv=n_kv)
            os_ = os_.reshape(ms, hq).astype(BF16)
            xp, xs, _ = matmul_residual(op, os_, w_mo, xp, xs, b_o[j][None, :], scale=1.0, tk=hq)
            zp3 = zp.reshape(batch, seq, qkv_dim)
            wp_ = min(WINDOW, seq)
            nkp.append(zp3[:, seq - wp_:, hq:hq + hk].reshape(batch, wp_, n_kv, HEAD_DIM))
            nvp.append(zp3[:, seq - wp_:, hq + hk:].reshape(batch, wp_, n_kv, HEAD_DIM))
            kk = jnp.concatenate([cache_k[j], kn.reshape(ms, 1, n_kv, HEAD_DIM)], axis=1)
            vv = jnp.concatenate([cache_v[j], vn.reshape(ms, 1, n_kv, HEAD_DIM)], axis=1)
            nks.append(kk[:, -wb:])
            nvs.append(vv[:, -wb:])
        else:
            (gbp, cup), (gbs, cus), _ = conv_gates(xp, xs, g[1], w_mi)
            tp = conv_mix_prompt(gbp, cup, conv_w[j], seq=seq, tm=_tile(seq, 512), tn=_tile(d, 1024))
            st = state_conv[j]
            ts = conv_mix_sample(gbs, cus, st[:, 0], st[:, 1], conv_w[j])
            xp, xs, _ = matmul_residual(tp, ts, w_mo, xp, xs, scale=1.0, tk=d)
            ncp.append(cup.reshape(batch, seq, d)[:, seq - (CONV_WIDTH - 1):])
            ncs.append(jnp.concatenate([st, cus[:, None, :]], axis=1)[:, -(CONV_WIDTH - 1):])

        next_up = [] if last else [job_gate(i + 1, 0), job_up(i + 1, 0)]
        act, act_s, (wd, w_pg, *wg_wu) = swiglu_up(xp, xs, g[2], wg_next, wu_next, d_ff=d_ff,
                                                   jobs=[job_down(i, 1), job_ple_gate(i)] + next_up)
        xp, xs, _ = matmul_residual(act, act_s, wd, xp, xs, scale=0.5, tk=tk_d, k_valid=d_ff)
        if not last:
            wg, wu = wg_wu

        xp, xs, _ = ple(xp, xs, g[3], w_pg, pp[i], ps[i], wpp[i])

    gf = final_norm_gain[None, :]
    y_prompt = rmsnorm(xp, gf, tm=_tile(mp, 256)).reshape(batch, seq, d)
    y_sample = rmsnorm(xs, gf, tm=ms).reshape(nseq, dec_seq, d)
    return (y_prompt, y_sample, jnp.stack(nkp), jnp.stack(nvp), jnp.stack(nks), jnp.stack(nvs),
            jnp.stack(ncp), jnp.stack(ncs))
```

```python
import functools
import math
from typing import NamedTuple

import jax
import jax.numpy as jnp
from jax import lax
from jax.experimental import pallas as pl
from jax.experimental.pallas import tpu as pltpu

NORM_EPS = 1e-6
NEG_INF = -1e30
HEAD_DIM = 64
WINDOW = 128
CONV_WIDTH = 3
PAST_LEN = 8192

V7X_VMEM_LIMIT_BYTES = 58 * 1024 * 1024
BF16_SUBLANES = 16

GROUP_ROWS = 1024
ROW_TILE = 512
NORM_ROWS = 256
PANEL_FFN = 512
PANEL_QKV = 512
PANEL_CONV = 256
PANEL_PLE = 512
PANEL_OUT = 1024
DOWN_ROWS = 1024
DOWN_K = 2816
SAMPLE_SEQS = 8

BF16 = jnp.bfloat16
F32 = jnp.float32


def _params(n_axes):
    return pltpu.CompilerParams(
        dimension_semantics=("arbitrary",) * n_axes,
        vmem_limit_bytes=V7X_VMEM_LIMIT_BYTES,
    )


def _tile(dim, target):
    if dim <= target:
        return dim
    t = target
    while dim % t:
        t //= 2
    return t


def _cdiv(a, b):
    return -(-a // b)


def _dot(a, b):
    return jnp.dot(a, b, preferred_element_type=F32)


class CastJob(NamedTuple):
    src: jax.Array
    lead: tuple
    tn: int
    rows_out: int


class _JobPlan(NamedTuple):
    rows: int
    cols: int
    tn: int
    panels: int
    rb: int
    nb_src: int
    nb_dst: int
    rows_out: int


def _plan_job(job, n_steps):
    rows, cols = job.src.shape[-2:]
    rb = BF16_SUBLANES
    while _cdiv(job.rows_out, rb) > n_steps:
        rb *= 2
    assert rows % rb == 0 and job.rows_out % rb == 0, (rows, job.rows_out, rb)
    return _JobPlan(rows, cols, job.tn, _cdiv(cols, job.tn), rb, rows // rb, job.rows_out // rb, job.rows_out)


def _job_specs(job, plan, step_of):
    n_lead = len(job.lead)

    def src_idx(*g):
        return (*job.lead, jnp.minimum(step_of(*g), plan.nb_src - 1), 0)

    def dst_idx(*g):
        return (0, jnp.minimum(step_of(*g), plan.nb_dst - 1), 0)

    src_spec = pl.BlockSpec((None,) * n_lead + (plan.rb, plan.cols), src_idx)
    dst_spec = pl.BlockSpec((plan.panels, plan.rb, plan.tn), dst_idx)
    dst_shape = jax.ShapeDtypeStruct((plan.panels, plan.rows_out, plan.tn), BF16)
    return src_spec, dst_spec, dst_shape


def _run_job(plan, src_ref, dst_ref, step):
    @pl.when(step < plan.nb_dst)
    def _():
        v = src_ref[...]
        if plan.rows_out > plan.rows:
            row = step * plan.rb + lax.broadcasted_iota(jnp.int32, (plan.rb, 1), 0)
            v = jnp.where(row < plan.rows, v, 0.0)
        for p in range(plan.panels):
            lo = p * plan.tn
            width = min(plan.tn, plan.cols - lo)
            dst_ref[p, :, 0:width] = v[:, lo:lo + width].astype(BF16)
            if width < plan.tn:
                dst_ref[p, :, width:plan.tn] = jnp.zeros((plan.rb, plan.tn - width), BF16)


def _call_with_jobs(body, *, name, grid, in_specs, out_specs, out_shape, scratch_shapes, args, jobs):
    n_steps = 1
    for gdim in grid:
        n_steps *= gdim

    def step_of(*g):
        s = g[0]
        for gdim, gi in zip(grid[1:], g[1:]):
            s = s * gdim + gi
        return s

    plans = [_plan_job(j, n_steps) for j in jobs]
    specs = [_job_specs(j, p, step_of) for j, p in zip(jobs, plans)]
    n_in, n_out, n_job = len(in_specs), len(out_specs), len(jobs)

    def kern(*refs):
        ins = refs[:n_in]
        job_src = refs[n_in:n_in + n_job]
        outs = refs[n_in + n_job:n_in + n_job + n_out]
        job_dst = refs[n_in + n_job + n_out:n_in + 2 * n_job + n_out]
        scratch = refs[n_in + 2 * n_job + n_out:]

        def run_jobs():
            if n_job:
                step = step_of(*[pl.program_id(a) for a in range(len(grid))])
                for plan, s_ref, d_ref in zip(plans, job_src, job_dst):
                    _run_job(plan, s_ref, d_ref, step)

        body(run_jobs, *ins, *outs, *scratch)

    res = pl.pallas_call(
        kern,
        grid=grid,
        in_specs=list(in_specs) + [s[0] for s in specs],
        out_specs=list(out_specs) + [s[1] for s in specs],
        out_shape=list(out_shape) + [s[2] for s in specs],
        scratch_shapes=scratch_shapes,
        compiler_params=_params(len(grid)),
        name=name,
    )(*args, *[j.src for j in jobs])
    return list(res[:n_out]), list(res[n_out:])


def cast_panels(job):
    n_steps = _cdiv(job.rows_out, 8 * BF16_SUBLANES)
    _, copies = _call_with_jobs(lambda run_jobs: run_jobs(), name="cast_panels", grid=(n_steps,), in_specs=[],
                                out_specs=[], out_shape=[], scratch_shapes=[], args=[], jobs=[job])
    return copies[0]


def _norm_rows_into(h_ref, row0, x_ref, g_ref):
    g = g_ref[...]
    chunk = BF16_SUBLANES

    def body(c, carry):
        r = pl.multiple_of(c * chunk, chunk)
        x = x_ref[pl.ds(r, chunk), :]
        y = x * lax.rsqrt(jnp.mean(x * x, axis=-1, keepdims=True) + NORM_EPS) * g
        h_ref[pl.ds(row0 + r, chunk), :] = y.astype(BF16)
        return carry

    n_chunks = x_ref.shape[0] // chunk
    lax.fori_loop(0, n_chunks, body, 0, unroll=min(n_chunks, 8))


class _Extra(NamedTuple):
    array: jax.Array
    cols: int
    by_panel: bool
    by_row: bool
    sample: object


def _norm_matmul(name, epilogue, x, g, panels, extras, outs, *, tn, n_panels, x_sample, last_cols=None, jobs=()):
    m, d = x.shape
    last_cols = tn if last_cols is None else last_cols
    assert last_cols == tn or not any(e.by_panel for e in extras)
    ms = x_sample.shape[0]
    group = _tile(m, GROUP_ROWS)
    tm = _tile(group, ROW_TILE)
    xb = _tile(group, NORM_ROWS)
    pre = group // xb
    n_groups = m // group
    last = n_groups - 1
    grid = (n_groups, pre + n_panels)
    n_p, n_e, n_o = len(panels), len(extras), len(outs)
    row_extras = [k for k, e in enumerate(extras) if e.by_row]

    def panel_of(n):
        return jnp.maximum(n - pre, 0)

    def sample_panel_of(i, n):
        return jnp.where(i == last, panel_of(n), 0)

    in_specs = [
        pl.BlockSpec((xb, d), lambda i, n: (i * pre + jnp.minimum(n, pre - 1), 0)),
        pl.BlockSpec((1, d), lambda i, n: (0, 0)),
    ]
    args = [x, g]
    for w, off in panels:
        in_specs.append(pl.BlockSpec((None, d, tn), functools.partial(lambda i, n, o: (panel_of(n) + o, 0, 0), o=off)))
        args.append(w)
    for e in extras:
        rows = group if e.by_row else e.array.shape[0]
        in_specs.append(pl.BlockSpec(
            (rows, e.cols),
            functools.partial(lambda i, n, r, c: (i if r else 0, panel_of(n) if c else 0), r=e.by_row, c=e.by_panel)))
        args.append(e.array)
    in_specs.append(pl.BlockSpec((ms, d), lambda i, n: (0, 0)))
    args.append(x_sample)
    for k in row_extras:
        e = extras[k]
        in_specs.append(pl.BlockSpec(
            (ms, e.cols), functools.partial(lambda i, n, c: (0, sample_panel_of(i, n) if c else 0), c=e.by_panel)))
        args.append(e.sample)
    n_in = len(in_specs)

    out_specs = [pl.BlockSpec((group, tn), lambda i, n: (i, panel_of(n))) for _ in outs]
    out_specs += [pl.BlockSpec((ms, tn), lambda i, n: (0, sample_panel_of(i, n))) for _ in outs]
    out_shape = [jax.ShapeDtypeStruct((m, cols), dt) for cols, dt in outs]
    out_shape += [jax.ShapeDtypeStruct((ms, cols), dt) for cols, dt in outs]

    def body(run_jobs, *refs):
        x_ref, g_ref = refs[:2]
        panel_refs = refs[2:2 + n_p]
        extra_refs = refs[2 + n_p:2 + n_p + n_e]
        xs_ref = refs[2 + n_p + n_e]
        extra_s_refs = dict(zip(row_extras, refs[3 + n_p + n_e:n_in]))
        out_refs = refs[n_in:n_in + n_o]
        out_s_refs = refs[n_in + n_o:n_in + 2 * n_o]
        h_ref = refs[n_in + 2 * n_o]
        i, n = pl.program_id(0), pl.program_id(1)

        @pl.when(n < pre)
        def _():
            _norm_rows_into(h_ref, pl.multiple_of(n * xb, xb), x_ref, g_ref)
            run_jobs()

        @pl.when((n == 0) & (i == last))
        def _():
            _norm_rows_into(h_ref, group, xs_ref, g_ref)

        def store(ref, rows, v):
            width = v.shape[1]
            ref[rows, 0:width] = v
            if width < tn:
                ref[rows, width:tn] = jnp.zeros((v.shape[0], tn - width), v.dtype)

        def panel_getter(cols):
            return lambda k: panel_refs[k][...] if cols == tn else panel_refs[k][:, 0:cols]

        def panel_step(cols, with_sample):
            n_sub = group // tm
            for s in range(n_sub):
                rows = slice(s * tm, (s + 1) * tm)
                joined = with_sample and s == n_sub - 1

                def get_extra(k, rows=rows, joined=joined):
                    if not extras[k].by_row:
                        return extra_refs[k][...]
                    v = extra_refs[k][rows, :]
                    return jnp.concatenate([v, extra_s_refs[k][...]], axis=0) if joined else v

                def put_out(k, v, rows=rows, joined=joined):
                    store(out_refs[k], rows, v[0:tm])
                    if joined:
                        store(out_s_refs[k], slice(None), v[tm:tm + ms])

                h_rows = slice(s * tm, group + ms) if joined else rows
                epilogue(h_ref[h_rows, :], panel_getter(cols), get_extra, put_out)
            run_jobs()

        last_panel = pre + n_panels - 1
        variants = [(n >= pre, tn)] if last_cols == tn else [((n >= pre) & (n < last_panel), tn),
                                                            (n == last_panel, last_cols)]
        for cond, cols in variants:
            pl.when(cond & (i < last))(functools.partial(panel_step, cols, False))
            pl.when(cond & (i == last))(functools.partial(panel_step, cols, True))

    outs_all, copies = _call_with_jobs(
        body, name=name, grid=grid, in_specs=in_specs, out_specs=out_specs, out_shape=out_shape,
        scratch_shapes=[pltpu.VMEM((group + ms, d), BF16)], args=args, jobs=list(jobs))
    return outs_all[:n_o], outs_all[n_o:], copies


def _swiglu_epilogue(h, panel, get_extra, put_out):
    gate = _dot(h, panel(0))
    up = _dot(h, panel(1))
    put_out(0, (gate * jax.nn.sigmoid(gate) * up).astype(BF16))


def swiglu_up(x, x_sample, g, wg, wu, *, d_ff, jobs=()):
    n_panels, _, tn = wg.shape
    outs, outs_s, copies = _norm_matmul(
        "swiglu_up", _swiglu_epilogue, x, g, [(wg, 0), (wu, 0)], [], [(n_panels * tn, BF16)],
        tn=tn, n_panels=n_panels, x_sample=x_sample, last_cols=d_ff - (n_panels - 1) * tn, jobs=jobs)
    return outs[0], outs_s[0], copies


def _linear_epilogue(h, panel, get_extra, put_out):
    put_out(0, _dot(h, panel(0)) + get_extra(0))


def norm_linear(x, x_sample, g, w, b, *, jobs=()):
    n_panels, _, tn = w.shape
    outs, outs_s, copies = _norm_matmul(
        "norm_linear", _linear_epilogue, x, g, [(w, 0)], [_Extra(b, tn, True, False, None)],
        [(n_panels * tn, F32)], tn=tn, n_panels=n_panels, x_sample=x_sample, jobs=jobs)
    return outs[0], outs_s[0], copies


def _conv_gates_epilogue(h, panel, get_extra, put_out):
    put_out(0, _dot(h, panel(0)))
    put_out(1, _dot(h, panel(1)) * _dot(h, panel(2)))


def conv_gates(x, x_sample, g, w_in, *, jobs=()):
    d = x.shape[1]
    tn = w_in.shape[2]
    n_panels = d // tn
    outs, outs_s, copies = _norm_matmul(
        "conv_gates", _conv_gates_epilogue, x, g, [(w_in, 0), (w_in, n_panels), (w_in, 2 * n_panels)], [],
        [(d, F32), (d, F32)], tn=tn, n_panels=n_panels, x_sample=x_sample, jobs=jobs)
    return outs, outs_s, copies


def _ple_epilogue(h, panel, get_extra, put_out):
    gate = jax.nn.sigmoid(_dot(h, panel(0)))
    proj = _dot(get_extra(1).astype(BF16), get_extra(2))
    put_out(0, get_extra(0) + gate * proj)


def ple(x, x_sample, g, w_gate, p, p_sample, w_proj, *, jobs=()):
    d = x.shape[1]
    pd = p.shape[1]
    n_panels, _, tn = w_gate.shape
    extras = [
        _Extra(x, tn, True, True, x_sample),
        _Extra(p, pd, False, True, p_sample),
        _Extra(w_proj, tn, True, False, None),
    ]
    outs, outs_s, copies = _norm_matmul(
        "ple", _ple_epilogue, x, g, [(w_gate, 0)], extras, [(d, F32)],
        tn=tn, n_panels=n_panels, x_sample=x_sample, jobs=jobs)
    return outs[0], outs_s[0], copies


class _RowSet(NamedTuple):
    a: object
    x: object
    o: object
    acc: object


def _residual_step(row_sets, w_ref, b_ref, *, scale, n_k, k, last_k_cols, then):
    tk, tn = w_ref.shape
    halves = [slice(c * (tn // 2), (c + 1) * (tn // 2)) for c in range(2)] if tn % 256 == 0 else [slice(0, tn)]

    def products(cols, k_cols):
        parts = [rs.a[...] if k_cols == tk else rs.a[:, 0:k_cols] for rs in row_sets]
        a = parts[0] if len(parts) == 1 else jnp.concatenate(parts, axis=0)
        y = _dot(a, w_ref[:, cols] if k_cols == tk else w_ref[0:k_cols, cols])
        out, r0 = [], 0
        for p in parts:
            out.append(y[r0:r0 + p.shape[0]])
            r0 += p.shape[0]
        return out

    def finish(rs, y, cols):
        if b_ref is not None:
            y = y + b_ref[:, cols]
        rs.o[:, cols] = rs.x[:, cols] + scale * y

    if n_k == 1:
        for cols in halves:
            for rs, y in zip(row_sets, products(cols, last_k_cols)):
                finish(rs, y, cols)
        then()
        return

    @pl.when(k < n_k - 1)
    def _():
        for cols in halves:
            for rs, y in zip(row_sets, products(cols, tk)):
                rs.acc[:, cols] += y
        then()

    @pl.when(k == n_k - 1)
    def _():
        for cols in halves:
            for rs, y in zip(row_sets, products(cols, last_k_cols)):
                acc = rs.acc[:, cols] + y
                rs.acc[:, cols] = jnp.zeros_like(acc)
                finish(rs, acc, cols)
        then()


def _matmul_residual_body(run_jobs, *refs, scale, has_bias, n_k, last_tile, last_k_cols):
    a_ref, w_ref, x_ref, as_ref, xs_ref = refs[:5]
    b_ref = refs[5] if has_bias else None
    o_ref, os_ref = refs[5 + has_bias:7 + has_bias]
    acc_ref, accs_ref = refs[7 + has_bias:9 + has_bias] if n_k > 1 else (None, None)
    i, j, k = pl.program_id(0), pl.program_id(1), pl.program_id(2)
    prompt = _RowSet(a_ref, x_ref, o_ref, acc_ref)
    sample = _RowSet(as_ref, xs_ref, os_ref, accs_ref)

    if n_k > 1:
        @pl.when((i == 0) & (j == 0) & (k == 0))
        def _():
            acc_ref[...] = jnp.zeros_like(acc_ref)

        @pl.when((i == last_tile) & (j == 0) & (k == 0))
        def _():
            accs_ref[...] = jnp.zeros_like(accs_ref)

    step = functools.partial(_residual_step, w_ref=w_ref, b_ref=b_ref, scale=scale, n_k=n_k, k=k,
                             last_k_cols=last_k_cols, then=run_jobs)
    pl.when(i < last_tile)(lambda: step([prompt]))
    pl.when(i == last_tile)(lambda: step([prompt, sample]))


def matmul_residual(a, a_sample, w, x, x_sample, b=None, *, scale, tk, k_valid=None, jobs=()):
    m, kdim = a.shape
    ms = a_sample.shape[0]
    n_panels, kw, tn = w.shape
    tm = _tile(m, DOWN_ROWS)
    assert kw == kdim and kdim % tk == 0 and m % tm == 0
    n_k = kdim // tk
    last_tile = m // tm - 1
    last_k_cols = tk if k_valid is None else k_valid - (n_k - 1) * tk
    assert 0 < last_k_cols <= tk and last_k_cols % 128 == 0

    def sample_col(i, j):
        return jnp.where(i == last_tile, j, 0)

    in_specs = [
        pl.BlockSpec((tm, tk), lambda i, j, k: (i, k)),
        pl.BlockSpec((None, tk, tn), lambda i, j, k: (j, k, 0)),
        pl.BlockSpec((tm, tn), lambda i, j, k: (i, j)),
        pl.BlockSpec((ms, tk), lambda i, j, k: (0, k)),
        pl.BlockSpec((ms, tn), lambda i, j, k: (0, sample_col(i, j))),
    ]
    args = [a, w, x, a_sample, x_sample]
    if b is not None:
        in_specs.append(pl.BlockSpec((1, tn), lambda i, j, k: (0, j)))
        args.append(b)
    scratch = [pltpu.VMEM((tm, tn), F32), pltpu.VMEM((ms, tn), F32)] if n_k > 1 else []
    outs, copies = _call_with_jobs(
        functools.partial(_matmul_residual_body, scale=scale, has_bias=b is not None, n_k=n_k, last_tile=last_tile,
                          last_k_cols=last_k_cols),
        name="matmul_residual", grid=(m // tm, n_panels, n_k), in_specs=in_specs,
        out_specs=[pl.BlockSpec((tm, tn), lambda i, j, k: (i, j)),
                   pl.BlockSpec((ms, tn), lambda i, j, k: (0, sample_col(i, j)))],
        out_shape=[jax.ShapeDtypeStruct((m, n_panels * tn), F32), jax.ShapeDtypeStruct((ms, n_panels * tn), F32)],
        scratch_shapes=scratch, args=args, jobs=list(jobs))
    return outs[0], outs[1], copies


def _rmsnorm_kernel(x_ref, g_ref, o_ref):
    x = x_ref[...]
    o_ref[...] = x * lax.rsqrt(jnp.mean(x * x, axis=-1, keepdims=True) + NORM_EPS) * g_ref[...]


def rmsnorm(x, g, *, tm):
    m, d = x.shape
    return pl.pallas_call(
        _rmsnorm_kernel,
        grid=(m // tm,),
        in_specs=[pl.BlockSpec((tm, d), lambda i: (i, 0)), pl.BlockSpec((1, d), lambda i: (0, 0))],
        out_specs=pl.BlockSpec((tm, d), lambda i: (i, 0)),
        out_shape=jax.ShapeDtypeStruct((m, d), F32),
        compiler_params=_params(1),
        name="rmsnorm",
    )(x, g)


def _attn_prompt_body(sink_ref, q_ref, kp_ref, kc_ref, vp_ref, vc_ref, o_ref, *, n_kv, group, scale):
    blk = pl.program_id(1)
    t = WINDOW
    pairs = group // 2
    q = (q_ref[...] * scale).astype(BF16)
    k_all = jnp.concatenate([kp_ref[...], kc_ref[...]], axis=0)
    v_all = jnp.concatenate([vp_ref[...], vc_ref[...]], axis=0)

    r = lax.broadcasted_iota(jnp.int32, (t, 2 * t), 0)
    c = lax.broadcasted_iota(jnp.int32, (t, 2 * t), 1)
    first_col = jnp.where(blk > 0, r, t)
    valid = (c >= first_col) & (c <= r + t)
    lane = lax.broadcasted_iota(jnp.int32, (2 * t, 2 * HEAD_DIM), 1)
    low = lane < HEAD_DIM

    for n in range(n_kv):
        slab = slice((n // 2) * 2 * HEAD_DIM, (n // 2 + 1) * 2 * HEAD_DIM)
        k2 = k_all[:, slab]
        v2 = v_all[:, slab]
        if n % 2 == 0:
            k_lo = jnp.where(low, k2, 0.0)
            v_lo = jnp.where(low, v2, 0.0)
            k_hi = pltpu.roll(k_lo, HEAD_DIM, 1)
            v_hi = pltpu.roll(v_lo, HEAD_DIM, 1)
        else:
            k_hi = jnp.where(low, 0.0, k2)
            v_hi = jnp.where(low, 0.0, v2)
            k_lo = pltpu.roll(k_hi, HEAD_DIM, 1)
            v_lo = pltpu.roll(v_hi, HEAD_DIM, 1)
        k_halves = (k_lo.astype(BF16), k_hi.astype(BF16))
        v_halves = (v_lo.astype(BF16), v_hi.astype(BF16))

        q_slabs = [q[:, (n * pairs + j) * 2 * HEAD_DIM:(n * pairs + j + 1) * 2 * HEAD_DIM] for j in range(pairs)]
        q4 = jnp.concatenate(q_slabs, axis=0)
        out = None
        for half in range(2):
            s = lax.dot_general(q4, k_halves[half], (((1,), (1,)), ((), ())),
                                preferred_element_type=F32)
            probs = []
            for j in range(pairs):
                sink = sink_ref[n * group + 2 * j + half]
                sj = jnp.where(valid, s[j * t:(j + 1) * t], NEG_INF)
                m = jnp.maximum(jnp.max(sj, axis=-1, keepdims=True), sink)
                pj = jnp.exp(sj - m)
                pj = pj / (jnp.sum(pj, axis=-1, keepdims=True) + jnp.exp(sink - m))
                probs.append(pj.astype(BF16))
            part = _dot(jnp.concatenate(probs, axis=0), v_halves[half])
            out = part if out is None else out + part
        for j in range(pairs):
            col = (n * pairs + j) * 2 * HEAD_DIM
            o_ref[:, col:col + 2 * HEAD_DIM] = out[j * t:(j + 1) * t].astype(o_ref.dtype)


def attn_prompt(z, sinks, *, batch, seq, n_heads, n_kv):
    t = WINDOW
    nb = seq // t
    hq = n_heads * HEAD_DIM
    hk = n_kv * HEAD_DIM
    kcol = hq // hk
    vcol = kcol + 1
    assert math.frexp(HEAD_DIM ** -0.5)[0] == 0.5

    def cur(b, i, s):
        return b * nb + i

    def prev(b, i, s):
        return b * nb + jnp.maximum(i - 1, 0)

    grid_spec = pltpu.PrefetchScalarGridSpec(
        num_scalar_prefetch=1,
        grid=(batch, nb),
        in_specs=[
            pl.BlockSpec((t, hq), lambda b, i, s: (cur(b, i, s), 0)),
            pl.BlockSpec((t, hk), lambda b, i, s: (prev(b, i, s), kcol)),
            pl.BlockSpec((t, hk), lambda b, i, s: (cur(b, i, s), kcol)),
            pl.BlockSpec((t, hk), lambda b, i, s: (prev(b, i, s), vcol)),
            pl.BlockSpec((t, hk), lambda b, i, s: (cur(b, i, s), vcol)),
        ],
        out_specs=pl.BlockSpec((t, hq), lambda b, i, s: (cur(b, i, s), 0)),
    )
    return pl.pallas_call(
        functools.partial(_attn_prompt_body, n_kv=n_kv, group=n_heads // n_kv, scale=HEAD_DIM ** -0.5),
        grid_spec=grid_spec,
        out_shape=jax.ShapeDtypeStruct((batch * seq, hq), BF16),
        compiler_params=_params(2),
        name="attn_prompt",
    )(sinks, z, z, z, z, z)


def _attn_sample_body(q_ref, kc_ref, vc_ref, kn_ref, vn_ref, sink_ref, o_ref, *, n_kv, group, scale, first_valid):
    n_heads, hd = q_ref.shape[1:]
    hk = n_kv * hd
    head = lax.broadcasted_iota(jnp.int32, (n_heads, hk), 0)
    lane = lax.broadcasted_iota(jnp.int32, (n_heads, hk), 1)
    own = (lane // hd) == (head // group)
    sink = sink_ref[...]
    for b in range(q_ref.shape[0]):
        q = q_ref[b]
        q2 = jnp.concatenate([q, q], axis=1)
        q_wide = jnp.where(own, jnp.concatenate([q2] * (n_kv // 2), axis=1), 0.0).astype(BF16)
        s_c = lax.dot_general(q_wide, kc_ref[b].astype(BF16), (((1,), (1,)), ((), ())),
                              preferred_element_type=F32) * scale
        kb = kn_ref[b].astype(BF16).astype(F32)
        s_n = jnp.sum(q_wide.astype(F32) * kb, axis=-1, keepdims=True) * scale
        if first_valid > 0:
            col = lax.broadcasted_iota(jnp.int32, s_c.shape, 1)
            s_c = jnp.where(col >= first_valid, s_c, NEG_INF)
        m = jnp.maximum(jnp.maximum(jnp.max(s_c, axis=-1, keepdims=True), s_n), sink)
        p_c = jnp.exp(s_c - m)
        p_n = jnp.exp(s_n - m)
        denom = jnp.sum(p_c, axis=-1, keepdims=True) + p_n + jnp.exp(sink - m)
        p_c = p_c / denom
        p_n = p_n / denom
        o_wide = _dot(p_c.astype(BF16), vc_ref[b].astype(BF16))
        o_wide = o_wide + p_n.astype(BF16).astype(F32) * vn_ref[b].astype(BF16).astype(F32)
        o_wide = jnp.where(own, o_wide, 0.0)
        o2 = o_wide[:, 0:2 * hd]
        for c in range(1, n_kv // 2):
            o2 = o2 + o_wide[:, c * 2 * hd:(c + 1) * 2 * hd]
        o_ref[b] = o2[:, 0:hd] + o2[:, hd:2 * hd]


def attn_sample(q, kc, vc, kn, vn, sinks, *, n_kv):
    nseq, n_heads, hd = q.shape
    wb = kc.shape[1]
    hk = kc.shape[2]
    first_valid = max(0, wb - WINDOW, wb - PAST_LEN)
    sb = _tile(nseq, SAMPLE_SEQS)
    return pl.pallas_call(
        functools.partial(_attn_sample_body, n_kv=n_kv, group=n_heads // n_kv, scale=HEAD_DIM ** -0.5,
                          first_valid=first_valid),
        grid=(nseq // sb,),
        in_specs=[
            pl.BlockSpec((sb, n_heads, hd), lambda b: (b, 0, 0)),
            pl.BlockSpec((sb, wb, hk), lambda b: (b, 0, 0)),
            pl.BlockSpec((sb, wb, hk), lambda b: (b, 0, 0)),
            pl.BlockSpec((sb, 1, hk), lambda b: (b, 0, 0)),
            pl.BlockSpec((sb, 1, hk), lambda b: (b, 0, 0)),
            pl.BlockSpec((n_heads, 1), lambda b: (0, 0)),
        ],
        out_specs=pl.BlockSpec((sb, n_heads, hd), lambda b: (b, 0, 0)),
        out_shape=jax.ShapeDtypeStruct((nseq, n_heads, hd), F32),
        compiler_params=_params(1),
        name="attn_sample",
    )(q, kc, vc, kn, vn, sinks)


def _conv_mix_prompt_body(gb_ref, cu_ref, halo_ref, w_ref, o_ref, pad_ref, *, tiles_per_seq):
    tm = cu_ref.shape[0]
    first = (pl.program_id(0) % tiles_per_seq) == 0
    halo = jnp.where(first, 0.0, halo_ref[...])
    pad_ref[0:8, :] = halo
    pad_ref[8:8 + tm, :] = cu_ref[...]
    w = w_ref[...]
    dw = w[0:1] * pad_ref[6:6 + tm, :] + w[1:2] * pad_ref[7:7 + tm, :] + w[2:3] * pad_ref[8:8 + tm, :]
    o_ref[...] = (gb_ref[...] * dw).astype(o_ref.dtype)


def conv_mix_prompt(gb, cu, w, *, seq, tm, tn):
    m, d = gb.shape
    hb = tm // 8
    return pl.pallas_call(
        functools.partial(_conv_mix_prompt_body, tiles_per_seq=seq // tm),
        grid=(m // tm, d // tn),
        in_specs=[
            pl.BlockSpec((tm, tn), lambda i, j: (i, j)),
            pl.BlockSpec((tm, tn), lambda i, j: (i, j)),
            pl.BlockSpec((8, tn), lambda i, j: (jnp.maximum(i * hb - 1, 0), j)),
            pl.BlockSpec((CONV_WIDTH, tn), lambda i, j: (0, j)),
        ],
        out_specs=pl.BlockSpec((tm, tn), lambda i, j: (i, j)),
        out_shape=jax.ShapeDtypeStruct((m, d), BF16),
        scratch_shapes=[pltpu.VMEM((tm + 8, tn), F32)],
        compiler_params=_params(2),
        name="conv_mix_prompt",
    )(gb, cu, cu, w)


def _conv_mix_sample_body(gb_ref, cu_ref, s0_ref, s1_ref, w_ref, o_ref):
    w = w_ref[...]
    dw = w[0:1] * s0_ref[...] + w[1:2] * s1_ref[...] + w[2:3] * cu_ref[...]
    o_ref[...] = (gb_ref[...] * dw).astype(o_ref.dtype)


def conv_mix_sample(gb, cu, s0, s1, w):
    return pl.pallas_call(
        _conv_mix_sample_body,
        out_shape=jax.ShapeDtypeStruct(gb.shape, BF16),
        name="conv_mix_sample",
    )(gb, cu, s0, s1, w)


def kernel(x_prompt, x_sample, cache_k, cache_v, state_conv, p_prompt, p_sample, norm_gains, final_norm_gain,
           w_ffn_gate, w_ffn_up, w_ffn_down, w_qkv, b_qkv, attn_sinks, w_o, b_o, w_conv_in, conv_w, w_conv_out,
           w_ple_proj, w_ple_gate):
    batch, seq, d = x_prompt.shape
    nseq, dec_seq, _ = x_sample.shape
    assert dec_seq == 1 and seq % WINDOW == 0 and CONV_WIDTH == conv_w.shape[1]
    depth = norm_gains.shape[0]
    d_ff = w_ffn_gate.shape[-1]
    n_heads = attn_sinks.shape[1]
    qkv_dim = w_qkv.shape[-1]
    n_kv = (qkv_dim // HEAD_DIM - n_heads) // 2
    hq, hk = n_heads * HEAD_DIM, n_kv * HEAD_DIM
    wb = cache_k.shape[2]
    mp, ms = batch * seq, nseq

    f_pad = _cdiv(d_ff, PANEL_FFN) * PANEL_FFN
    tk_d = _tile(f_pad, DOWN_K)
    panel_qkv = _tile(qkv_dim, PANEL_QKV)
    panel_conv = _tile(d, PANEL_CONV)
    panel_ple = _tile(d, PANEL_PLE)
    panel_out = _tile(d, PANEL_OUT)

    def job_gate(i, h):
        return CastJob(w_ffn_gate, (i, h), PANEL_FFN, d)

    def job_up(i, h):
        return CastJob(w_ffn_up, (i, h), PANEL_FFN, d)

    def job_down(i, h):
        return CastJob(w_ffn_down, (i, h), panel_out, f_pad)

    def job_mixer_in(i):
        if i % 2 == 0:
            return CastJob(w_qkv, (i // 2,), panel_qkv, d)
        return CastJob(w_conv_in, (i // 2,), panel_conv, d)

    def job_mixer_out(i):
        if i % 2 == 0:
            return CastJob(w_o, (i // 2,), panel_out, hq)
        return CastJob(w_conv_out, (i // 2,), panel_out, d)

    def job_ple_gate(i):
        return CastJob(w_ple_gate, (i,), panel_ple, d)

    wpp = w_ple_proj.astype(BF16)

    xp = x_prompt.reshape(mp, d)
    xs = x_sample.reshape(ms, d)
    pp = p_prompt.reshape(depth, mp, -1)
    ps = p_sample.reshape(depth, ms, -1)

    wg = cast_panels(job_gate(0, 0))
    wu = cast_panels(job_up(0, 0))

    nkp, nvp, nks, nvs, ncp, ncs = [], [], [], [], [], []
    for i in range(depth):
        g = norm_gains[i][:, None, :]
        last = i == depth - 1
        j = i // 2

        act, act_s, (wd, w_mi, wu_next, wg_next) = swiglu_up(
            xp, xs, g[0], wg, wu, d_ff=d_ff,
            jobs=[job_down(i, 0), job_mixer_in(i), job_up(i, 1), job_gate(i, 1)])
        xp, xs, (w_mo,) = matmul_residual(act, act_s, wd, xp, xs, scale=0.5, tk=tk_d, k_valid=d_ff,
                                          jobs=[job_mixer_out(i)])

        if i % 2 == 0:
            zp, zs, _ = norm_linear(xp, xs, g[1], w_mi, b_qkv[j][None, :])
            op = attn_prompt(zp, attn_sinks[j], batch=batch, seq=seq, n_heads=n_heads, n_kv=n_kv)
            kn = zs[:, hq:hq + hk]
            vn = zs[:, hq + hk:]
            os_ = attn_sample(zs[:, :hq].reshape(ms, n_heads, HEAD_DIM),
                              cache_k[j].reshape(ms, wb, hk), cache_v[j].reshape(ms, wb, hk),
                              kn[:, None, :], vn[:, None, :], attn_sinks[j][:, None],
                              n_kv=n_kv)
            os_ = os_.reshape(ms, hq).astype(BF16)
            xp, xs, _ = matmul_residual(op, os_, w_mo, xp, xs, b_o[j][None, :], scale=1.0, tk=hq)
            zp3 = zp.reshape(batch, seq, qkv_dim)
            wp_ = min(WINDOW, seq)
            nkp.append(zp3[:, seq - wp_:, hq:hq + hk].reshape(batch, wp_, n_kv, HEAD_DIM))
            nvp.append(zp3[:, seq - wp_:, hq + hk:].reshape(batch, wp_, n_kv, HEAD_DIM))
            kk = jnp.concatenate([cache_k[j], kn.reshape(ms, 1, n_kv, HEAD_DIM)], axis=1)
            vv = jnp.concatenate([cache_v[j], vn.reshape(ms, 1, n_kv, HEAD_DIM)], axis=1)
            nks.append(kk[:, -wb:])
            nvs.append(vv[:, -wb:])
        else:
            (gbp, cup), (gbs, cus), _ = conv_gates(xp, xs, g[1], w_mi)
            tp = conv_mix_prompt(gbp, cup, conv_w[j], seq=seq, tm=_tile(seq, 512), tn=_tile(d, 1024))
            st = state_conv[j]
            ts = conv_mix_sample(gbs, cus, st[:, 0], st[:, 1], conv_w[j])
            xp, xs, _ = matmul_residual(tp, ts, w_mo, xp, xs, scale=1.0, tk=d)
            ncp.append(cup.reshape(batch, seq, d)[:, seq - (CONV_WIDTH - 1):])
            ncs.append(jnp.concatenate([st, cus[:, None, :]], axis=1)[:, -(CONV_WIDTH - 1):])

        next_up = [] if last else [job_gate(i + 1, 0), job_up(i + 1, 0)]
        act, act_s, (wd, w_pg, *wg_wu) = swiglu_up(xp, xs, g[2], wg_next, wu_next, d_ff=d_ff,
                                                   jobs=[job_down(i, 1), job_ple_gate(i)] + next_up)
        xp, xs, _ = matmul_residual(act, act_s, wd, xp, xs, scale=0.5, tk=tk_d, k_valid=d_ff)
        if not last:
            wg, wu = wg_wu

        xp, xs, _ = ple(xp, xs, g[3], w_pg, pp[i], ps[i], wpp[i])

    gf = final_norm_gain[None, :]
    y_prompt = rmsnorm(xp, gf, tm=_tile(mp, 256)).reshape(batch, seq, d)
    y_sample = rmsnorm(xs, gf, tm=ms).reshape(nseq, dec_seq, d)
    return (y_prompt, y_sample, jnp.stack(nkp), jnp.stack(nvp), jnp.stack(nks), jnp.stack(nvs),
            jnp.stack(ncp), jnp.stack(ncs))
```

```python
import functools
import math
from typing import NamedTuple

import jax
import jax.numpy as jnp
from jax import lax
from jax.experimental import pallas as pl
from jax.experimental.pallas import tpu as pltpu

NORM_EPS = 1e-6
NEG_INF = -1e30
HEAD_DIM = 64
WINDOW = 128
CONV_WIDTH = 3
PAST_LEN = 8192

V7X_VMEM_LIMIT_BYTES = 58 * 1024 * 1024
BF16_SUBLANES = 16

GROUP_ROWS = 1024
WIDE_GROUP_ROWS = 2048
ROW_TILE = 512
NORM_ROWS = 256
PANEL_FFN = 512
PANEL_QKV = 512
PANEL_CONV = 256
PANEL_PLE = 512
PANEL_OUT = 1024
DOWN_ROWS = 1024
DOWN_K = 2816
SAMPLE_SEQS = 8

BF16 = jnp.bfloat16
F32 = jnp.float32


def _params(n_axes):
    return pltpu.CompilerParams(
        dimension_semantics=("arbitrary",) * n_axes,
        vmem_limit_bytes=V7X_VMEM_LIMIT_BYTES,
    )


def _tile(dim, target):
    if dim <= target:
        return dim
    t = target
    while dim % t:
        t //= 2
    return t


def _cdiv(a, b):
    return -(-a // b)


def _dot(a, b):
    return jnp.dot(a, b, preferred_element_type=F32)


class CastJob(NamedTuple):
    src: jax.Array
    lead: tuple
    tn: int
    rows_out: int


class _JobPlan(NamedTuple):
    rows: int
    cols: int
    tn: int
    panels: int
    rb: int
    nb_src: int
    nb_dst: int
    rows_out: int


def _plan_job(job, n_steps):
    rows, cols = job.src.shape[-2:]
    rb = BF16_SUBLANES
    while _cdiv(job.rows_out, rb) > n_steps:
        rb *= 2
    assert rows % rb == 0 and job.rows_out % rb == 0, (rows, job.rows_out, rb)
    return _JobPlan(rows, cols, job.tn, _cdiv(cols, job.tn), rb, rows // rb, job.rows_out // rb, job.rows_out)


def _job_specs(job, plan, step_of):
    n_lead = len(job.lead)

    def src_idx(*g):
        return (*job.lead, jnp.minimum(step_of(*g), plan.nb_src - 1), 0)

    def dst_idx(*g):
        return (0, jnp.minimum(step_of(*g), plan.nb_dst - 1), 0)

    src_spec = pl.BlockSpec((None,) * n_lead + (plan.rb, plan.cols), src_idx)
    dst_spec = pl.BlockSpec((plan.panels, plan.rb, plan.tn), dst_idx)
    dst_shape = jax.ShapeDtypeStruct((plan.panels, plan.rows_out, plan.tn), BF16)
    return src_spec, dst_spec, dst_shape


def _run_job(plan, src_ref, dst_ref, step):
    @pl.when(step < plan.nb_dst)
    def _():
        v = src_ref[...]
        if plan.rows_out > plan.rows:
            row = step * plan.rb + lax.broadcasted_iota(jnp.int32, (plan.rb, 1), 0)
            v = jnp.where(row < plan.rows, v, 0.0)
        for p in range(plan.panels):
            lo = p * plan.tn
            width = min(plan.tn, plan.cols - lo)
            dst_ref[p, :, 0:width] = v[:, lo:lo + width].astype(BF16)
            if width < plan.tn:
                dst_ref[p, :, width:plan.tn] = jnp.zeros((plan.rb, plan.tn - width), BF16)


def _call_with_jobs(body, *, name, grid, in_specs, out_specs, out_shape, scratch_shapes, args, jobs):
    n_steps = 1
    for gdim in grid:
        n_steps *= gdim

    def step_of(*g):
        s = g[0]
        for gdim, gi in zip(grid[1:], g[1:]):
            s = s * gdim + gi
        return s

    plans = [_plan_job(j, n_steps) for j in jobs]
    specs = [_job_specs(j, p, step_of) for j, p in zip(jobs, plans)]
    n_in, n_out, n_job = len(in_specs), len(out_specs), len(jobs)

    def kern(*refs):
        ins = refs[:n_in]
        job_src = refs[n_in:n_in + n_job]
        outs = refs[n_in + n_job:n_in + n_job + n_out]
        job_dst = refs[n_in + n_job + n_out:n_in + 2 * n_job + n_out]
        scratch = refs[n_in + 2 * n_job + n_out:]

        def run_jobs():
            if n_job:
                step = step_of(*[pl.program_id(a) for a in range(len(grid))])
                for plan, s_ref, d_ref in zip(plans, job_src, job_dst):
                    _run_job(plan, s_ref, d_ref, step)

        body(run_jobs, *ins, *outs, *scratch)

    res = pl.pallas_call(
        kern,
        grid=grid,
        in_specs=list(in_specs) + [s[0] for s in specs],
        out_specs=list(out_specs) + [s[1] for s in specs],
        out_shape=list(out_shape) + [s[2] for s in specs],
        scratch_shapes=scratch_shapes,
        compiler_params=_params(len(grid)),
        name=name,
    )(*args, *[j.src for j in jobs])
    return list(res[:n_out]), list(res[n_out:])


def cast_panels(job):
    n_steps = _cdiv(job.rows_out, 8 * BF16_SUBLANES)
    _, copies = _call_with_jobs(lambda run_jobs: run_jobs(), name="cast_panels", grid=(n_steps,), in_specs=[],
                                out_specs=[], out_shape=[], scratch_shapes=[], args=[], jobs=[job])
    return copies[0]


def _norm_rows_into(h_ref, row0, x_ref, g_ref):
    g = g_ref[...]
    chunk = BF16_SUBLANES

    def body(c, carry):
        r = pl.multiple_of(c * chunk, chunk)
        x = x_ref[pl.ds(r, chunk), :]
        y = x * lax.rsqrt(jnp.mean(x * x, axis=-1, keepdims=True) + NORM_EPS) * g
        h_ref[pl.ds(row0 + r, chunk), :] = y.astype(BF16)
        return carry

    n_chunks = x_ref.shape[0] // chunk
    lax.fori_loop(0, n_chunks, body, 0, unroll=min(n_chunks, 8))


class _Extra(NamedTuple):
    array: jax.Array
    cols: int
    by_panel: bool
    by_row: bool
    sample: object


class _TileCtx(NamedTuple):
    s: int
    n_sub: int
    tm: int
    group: object
    panel: object
    scratch: tuple


def _norm_matmul(name, epilogue, x, g, panels, extras, outs, *, tn, n_panels, x_sample, last_cols=None,
                 group_rows=GROUP_ROWS, scratch=(), jobs=()):
    m, d = x.shape
    last_cols = tn if last_cols is None else last_cols
    assert last_cols == tn or not any(e.by_panel for e in extras)
    ms = x_sample.shape[0]
    group = _tile(m, group_rows)
    tm = _tile(group, ROW_TILE)
    xb = _tile(group, NORM_ROWS)
    pre = group // xb
    n_groups = m // group
    last = n_groups - 1
    grid = (n_groups, pre + n_panels)
    n_p, n_e, n_o = len(panels), len(extras), len(outs)
    row_extras = [k for k, e in enumerate(extras) if e.by_row]

    def panel_of(n):
        return jnp.maximum(n - pre, 0)

    def sample_panel_of(i, n):
        return jnp.where(i == last, panel_of(n), 0)

    in_specs = [
        pl.BlockSpec((xb, d), lambda i, n: (i * pre + jnp.minimum(n, pre - 1), 0)),
        pl.BlockSpec((1, d), lambda i, n: (0, 0)),
    ]
    args = [x, g]
    for w, off in panels:
        in_specs.append(pl.BlockSpec((None, d, tn), functools.partial(lambda i, n, o: (panel_of(n) + o, 0, 0), o=off)))
        args.append(w)
    for e in extras:
        rows = group if e.by_row else e.array.shape[0]
        in_specs.append(pl.BlockSpec(
            (rows, e.cols),
            functools.partial(lambda i, n, r, c: (i if r else 0, panel_of(n) if c else 0), r=e.by_row, c=e.by_panel)))
        args.append(e.array)
    in_specs.append(pl.BlockSpec((ms, d), lambda i, n: (0, 0)))
    args.append(x_sample)
    for k in row_extras:
        e = extras[k]
        in_specs.append(pl.BlockSpec(
            (ms, e.cols), functools.partial(lambda i, n, c: (0, sample_panel_of(i, n) if c else 0), c=e.by_panel)))
        args.append(e.sample)
    n_in = len(in_specs)

    out_specs = [pl.BlockSpec((group, tn), lambda i, n: (i, panel_of(n))) for _ in outs]
    out_specs += [pl.BlockSpec((ms, tn), lambda i, n: (0, sample_panel_of(i, n))) for _ in outs]
    out_shape = [jax.ShapeDtypeStruct((m, cols), dt) for cols, dt in outs]
    out_shape += [jax.ShapeDtypeStruct((ms, cols), dt) for cols, dt in outs]

    def body(run_jobs, *refs):
        x_ref, g_ref = refs[:2]
        panel_refs = refs[2:2 + n_p]
        extra_refs = refs[2 + n_p:2 + n_p + n_e]
        xs_ref = refs[2 + n_p + n_e]
        extra_s_refs = dict(zip(row_extras, refs[3 + n_p + n_e:n_in]))
        out_refs = refs[n_in:n_in + n_o]
        out_s_refs = refs[n_in + n_o:n_in + 2 * n_o]
        h_ref = refs[n_in + 2 * n_o]
        extra_scratch = tuple(refs[n_in + 2 * n_o + 1:])
        i, n = pl.program_id(0), pl.program_id(1)

        @pl.when(n < pre)
        def _():
            _norm_rows_into(h_ref, pl.multiple_of(n * xb, xb), x_ref, g_ref)
            run_jobs()

        @pl.when((n == 0) & (i == last))
        def _():
            _norm_rows_into(h_ref, group, xs_ref, g_ref)

        def store(ref, rows, v):
            width = v.shape[1]
            ref[rows, 0:width] = v
            if width < tn:
                ref[rows, width:tn] = jnp.zeros((v.shape[0], tn - width), v.dtype)

        def panel_getter(cols):
            return lambda k: panel_refs[k][...] if cols == tn else panel_refs[k][:, 0:cols]

        def panel_step(cols, with_sample):
            n_sub = group // tm
            for s in range(n_sub):
                rows = slice(s * tm, (s + 1) * tm)
                joined = with_sample and s == n_sub - 1

                def get_extra(k, rows=rows, joined=joined):
                    if not extras[k].by_row:
                        return extra_refs[k][...]
                    v = extra_refs[k][rows, :]
                    return jnp.concatenate([v, extra_s_refs[k][...]], axis=0) if joined else v

                def put_out(k, v, rows=rows, joined=joined):
                    store(out_refs[k], rows, v[0:tm])
                    if joined:
                        store(out_s_refs[k], slice(None), v[tm:tm + ms])

                h_rows = slice(s * tm, group + ms) if joined else rows
                ctx = _TileCtx(s, n_sub, tm, i, n - pre, extra_scratch)
                epilogue(h_ref[h_rows, :], panel_getter(cols), get_extra, put_out, ctx)
            run_jobs()

        last_panel = pre + n_panels - 1
        variants = [(n >= pre, tn)] if last_cols == tn else [((n >= pre) & (n < last_panel), tn),
                                                            (n == last_panel, last_cols)]
        for cond, cols in variants:
            pl.when(cond & (i < last))(functools.partial(panel_step, cols, False))
            pl.when(cond & (i == last))(functools.partial(panel_step, cols, True))

    outs_all, copies = _call_with_jobs(
        body, name=name, grid=grid, in_specs=in_specs, out_specs=out_specs, out_shape=out_shape,
        scratch_shapes=[pltpu.VMEM((group + ms, d), BF16)] + [f(tm, tn) for f in scratch], args=args,
        jobs=list(jobs))
    return outs_all[:n_o], outs_all[n_o:], copies


def _swiglu_epilogue(h, panel, get_extra, put_out, ctx):
    gate = _dot(h, panel(0))
    up = _dot(h, panel(1))
    put_out(0, (gate * jax.nn.sigmoid(gate) * up).astype(BF16))


def swiglu_up(x, x_sample, g, wg, wu, *, d_ff, jobs=()):
    n_panels, _, tn = wg.shape
    outs, outs_s, copies = _norm_matmul(
        "swiglu_up", _swiglu_epilogue, x, g, [(wg, 0), (wu, 0)], [], [(n_panels * tn, BF16)],
        tn=tn, n_panels=n_panels, x_sample=x_sample, last_cols=d_ff - (n_panels - 1) * tn, jobs=jobs)
    return outs[0], outs_s[0], copies


def _linear_epilogue(h, panel, get_extra, put_out, ctx):
    put_out(0, _dot(h, panel(0)) + get_extra(0))


def norm_linear(x, x_sample, g, w, b, *, jobs=()):
    n_panels, _, tn = w.shape
    outs, outs_s, copies = _norm_matmul(
        "norm_linear", _linear_epilogue, x, g, [(w, 0)], [_Extra(b, tn, True, False, None)],
        [(n_panels * tn, F32)], tn=tn, n_panels=n_panels, x_sample=x_sample, group_rows=WIDE_GROUP_ROWS,
        jobs=jobs)
    return outs[0], outs_s[0], copies


def _conv_mixer_epilogue(h, panel, get_extra, put_out, ctx, *, groups_per_seq):
    pad_ref, carry_ref = ctx.scratch
    tm = ctx.tm
    gb = _dot(h, panel(0))
    cu = _dot(h, panel(1)) * _dot(h, panel(2))
    w = get_extra(0)
    if ctx.s > 0:
        halo = pad_ref[tm:tm + 8, :]
    elif groups_per_seq == 1:
        halo = jnp.zeros((8, cu.shape[1]), F32)
    else:
        halo = jnp.where(ctx.group % groups_per_seq == 0, 0.0, carry_ref[ctx.panel])
    pad_ref[0:8, :] = halo
    pad_ref[8:8 + tm, :] = cu[0:tm]
    dw = w[0:1] * pad_ref[6:6 + tm, :] + w[1:2] * pad_ref[7:7 + tm, :] + w[2:3] * cu[0:tm]
    if ctx.s == ctx.n_sub - 1 and groups_per_seq > 1:
        carry_ref[ctx.panel] = cu[tm - 8:tm]
    mixed = (gb[0:tm] * dw).astype(BF16)
    extra_rows = h.shape[0] - tm
    if extra_rows:
        mixed = jnp.concatenate([mixed, jnp.zeros((extra_rows, mixed.shape[1]), BF16)], axis=0)
    put_out(0, mixed)
    put_out(1, cu)
    put_out(2, gb)


def conv_mixer_in(x, x_sample, g, w_in, conv_w, *, seq, jobs=()):
    m, d = x.shape
    tn = w_in.shape[2]
    n_panels = d // tn
    group = _tile(m, WIDE_GROUP_ROWS)
    assert seq % group == 0 and CONV_WIDTH - 1 <= 8
    outs, outs_s, copies = _norm_matmul(
        "conv_mixer_in", functools.partial(_conv_mixer_epilogue, groups_per_seq=seq // group), x, g,
        [(w_in, 0), (w_in, n_panels), (w_in, 2 * n_panels)], [_Extra(conv_w, tn, True, False, None)],
        [(d, BF16), (d, F32), (d, F32)], tn=tn, n_panels=n_panels, x_sample=x_sample,
        group_rows=WIDE_GROUP_ROWS,
        scratch=[lambda tm, tn: pltpu.VMEM((tm + 16, tn), F32),
                 lambda tm, tn: pltpu.VMEM((n_panels, 8, tn), F32)],
        jobs=jobs)
    return outs, outs_s, copies


def _ple_epilogue(h, panel, get_extra, put_out, ctx):
    gate = jax.nn.sigmoid(_dot(h, panel(0)))
    proj = _dot(get_extra(1).astype(BF16), get_extra(2))
    put_out(0, get_extra(0) + gate * proj)


def ple(x, x_sample, g, w_gate, p, p_sample, w_proj, *, jobs=()):
    d = x.shape[1]
    pd = p.shape[1]
    n_panels, _, tn = w_gate.shape
    extras = [
        _Extra(x, tn, True, True, x_sample),
        _Extra(p, pd, False, True, p_sample),
        _Extra(w_proj, tn, True, False, None),
    ]
    outs, outs_s, copies = _norm_matmul(
        "ple", _ple_epilogue, x, g, [(w_gate, 0)], extras, [(d, F32)],
        tn=tn, n_panels=n_panels, x_sample=x_sample, jobs=jobs)
    return outs[0], outs_s[0], copies


class _RowSet(NamedTuple):
    a: object
    x: object
    o: object
    acc: object


def _residual_step(row_sets, w_ref, b_ref, *, scale, n_k, k, last_k_cols, then):
    tk, tn = w_ref.shape
    halves = [slice(c * (tn // 2), (c + 1) * (tn // 2)) for c in range(2)] if tn % 256 == 0 else [slice(0, tn)]

    def products(cols, k_cols):
        parts = [rs.a[...] if k_cols == tk else rs.a[:, 0:k_cols] for rs in row_sets]
        a = parts[0] if len(parts) == 1 else jnp.concatenate(parts, axis=0)
        y = _dot(a, w_ref[:, cols] if k_cols == tk else w_ref[0:k_cols, cols])
        out, r0 = [], 0
        for p in parts:
            out.append(y[r0:r0 + p.shape[0]])
            r0 += p.shape[0]
        return out

    def finish(rs, y, cols):
        if b_ref is not None:
            y = y + b_ref[:, cols]
        rs.o[:, cols] = rs.x[:, cols] + scale * y

    if n_k == 1:
        for cols in halves:
            for rs, y in zip(row_sets, products(cols, last_k_cols)):
                finish(rs, y, cols)
        then()
        return

    @pl.when(k < n_k - 1)
    def _():
        for cols in halves:
            for rs, y in zip(row_sets, products(cols, tk)):
                rs.acc[:, cols] += y
        then()

    @pl.when(k == n_k - 1)
    def _():
        for cols in halves:
            for rs, y in zip(row_sets, products(cols, last_k_cols)):
                acc = rs.acc[:, cols] + y
                rs.acc[:, cols] = jnp.zeros_like(acc)
                finish(rs, acc, cols)
        then()


def _matmul_residual_body(run_jobs, *refs, scale, has_bias, n_k, last_tile, last_k_cols):
    a_ref, w_ref, x_ref, as_ref, xs_ref = refs[:5]
    b_ref = refs[5] if has_bias else None
    o_ref, os_ref = refs[5 + has_bias:7 + has_bias]
    acc_ref, accs_ref = refs[7 + has_bias:9 + has_bias] if n_k > 1 else (None, None)
    i, j, k = pl.program_id(0), pl.program_id(1), pl.program_id(2)
    prompt = _RowSet(a_ref, x_ref, o_ref, acc_ref)
    sample = _RowSet(as_ref, xs_ref, os_ref, accs_ref)

    if n_k > 1:
        @pl.when((i == 0) & (j == 0) & (k == 0))
        def _():
            acc_ref[...] = jnp.zeros_like(acc_ref)

        @pl.when((i == last_tile) & (j == 0) & (k == 0))
        def _():
            accs_ref[...] = jnp.zeros_like(accs_ref)

    step = functools.partial(_residual_step, w_ref=w_ref, b_ref=b_ref, scale=scale, n_k=n_k, k=k,
                             last_k_cols=last_k_cols, then=run_jobs)
    pl.when(i < last_tile)(lambda: step([prompt]))
    pl.when(i == last_tile)(lambda: step([prompt, sample]))


def matmul_residual(a, a_sample, w, x, x_sample, b=None, *, scale, tk, k_valid=None, jobs=()):
    m, kdim = a.shape
    ms = a_sample.shape[0]
    n_panels, kw, tn = w.shape
    tm = _tile(m, DOWN_ROWS)
    assert kw == kdim and kdim % tk == 0 and m % tm == 0
    n_k = kdim // tk
    last_tile = m // tm - 1
    last_k_cols = tk if k_valid is None else k_valid - (n_k - 1) * tk
    assert 0 < last_k_cols <= tk and last_k_cols % 128 == 0

    def sample_col(i, j):
        return jnp.where(i == last_tile, j, 0)

    in_specs = [
        pl.BlockSpec((tm, tk), lambda i, j, k: (i, k)),
        pl.BlockSpec((None, tk, tn), lambda i, j, k: (j, k, 0)),
        pl.BlockSpec((tm, tn), lambda i, j, k: (i, j)),
        pl.BlockSpec((ms, tk), lambda i, j, k: (0, k)),
        pl.BlockSpec((ms, tn), lambda i, j, k: (0, sample_col(i, j))),
    ]
    args = [a, w, x, a_sample, x_sample]
    if b is not None:
        in_specs.append(pl.BlockSpec((1, tn), lambda i, j, k: (0, j)))
        args.append(b)
    scratch = [pltpu.VMEM((tm, tn), F32), pltpu.VMEM((ms, tn), F32)] if n_k > 1 else []
    outs, copies = _call_with_jobs(
        functools.partial(_matmul_residual_body, scale=scale, has_bias=b is not None, n_k=n_k, last_tile=last_tile,
                          last_k_cols=last_k_cols),
        name="matmul_residual", grid=(m // tm, n_panels, n_k), in_specs=in_specs,
        out_specs=[pl.BlockSpec((tm, tn), lambda i, j, k: (i, j)),
                   pl.BlockSpec((ms, tn), lambda i, j, k: (0, sample_col(i, j)))],
        out_shape=[jax.ShapeDtypeStruct((m, n_panels * tn), F32), jax.ShapeDtypeStruct((ms, n_panels * tn), F32)],
        scratch_shapes=scratch, args=args, jobs=list(jobs))
    return outs[0], outs[1], copies


def _rmsnorm_kernel(x_ref, g_ref, o_ref):
    x = x_ref[...]
    o_ref[...] = x * lax.rsqrt(jnp.mean(x * x, axis=-1, keepdims=True) + NORM_EPS) * g_ref[...]


def rmsnorm(x, g, *, tm):
    m, d = x.shape
    return pl.pallas_call(
        _rmsnorm_kernel,
        grid=(m // tm,),
        in_specs=[pl.BlockSpec((tm, d), lambda i: (i, 0)), pl.BlockSpec((1, d), lambda i: (0, 0))],
        out_specs=pl.BlockSpec((tm, d), lambda i: (i, 0)),
        out_shape=jax.ShapeDtypeStruct((m, d), F32),
        compiler_params=_params(1),
        name="rmsnorm",
    )(x, g)


def _attn_prompt_body(sink_ref, q_ref, kp_ref, kc_ref, vp_ref, vc_ref, o_ref, *, n_kv, group, scale):
    blk = pl.program_id(1)
    t = WINDOW
    pairs = group // 2
    q = (q_ref[...] * scale).astype(BF16)
    k_all = jnp.concatenate([kp_ref[...], kc_ref[...]], axis=0)
    v_all = jnp.concatenate([vp_ref[...], vc_ref[...]], axis=0)

    r = lax.broadcasted_iota(jnp.int32, (t, 2 * t), 0)
    c = lax.broadcasted_iota(jnp.int32, (t, 2 * t), 1)
    first_col = jnp.where(blk > 0, r, t)
    valid = (c >= first_col) & (c <= r + t)
    lane = lax.broadcasted_iota(jnp.int32, (2 * t, 2 * HEAD_DIM), 1)
    low = lane < HEAD_DIM

    for n in range(n_kv):
        slab = slice((n // 2) * 2 * HEAD_DIM, (n // 2 + 1) * 2 * HEAD_DIM)
        k2 = k_all[:, slab]
        v2 = v_all[:, slab]
        if n % 2 == 0:
            k_lo = jnp.where(low, k2, 0.0)
            v_lo = jnp.where(low, v2, 0.0)
            k_hi = pltpu.roll(k_lo, HEAD_DIM, 1)
            v_hi = pltpu.roll(v_lo, HEAD_DIM, 1)
        else:
            k_hi = jnp.where(low, 0.0, k2)
            v_hi = jnp.where(low, 0.0, v2)
            k_lo = pltpu.roll(k_hi, HEAD_DIM, 1)
            v_lo = pltpu.roll(v_hi, HEAD_DIM, 1)
        k_halves = (k_lo.astype(BF16), k_hi.astype(BF16))
        v_halves = (v_lo.astype(BF16), v_hi.astype(BF16))

        q_slabs = [q[:, (n * pairs + j) * 2 * HEAD_DIM:(n * pairs + j + 1) * 2 * HEAD_DIM] for j in range(pairs)]
        q4 = jnp.concatenate(q_slabs, axis=0)
        out = None
        for half in range(2):
            s = lax.dot_general(q4, k_halves[half], (((1,), (1,)), ((), ())),
                                preferred_element_type=F32)
            probs = []
            for j in range(pairs):
                sink = sink_ref[n * group + 2 * j + half]
                sj = jnp.where(valid, s[j * t:(j + 1) * t], NEG_INF)
                m = jnp.maximum(jnp.max(sj, axis=-1, keepdims=True), sink)
                pj = jnp.exp(sj - m)
                pj = pj / (jnp.sum(pj, axis=-1, keepdims=True) + jnp.exp(sink - m))
                probs.append(pj.astype(BF16))
            part = _dot(jnp.concatenate(probs, axis=0), v_halves[half])
            out = part if out is None else out + part
        for j in range(pairs):
            col = (n * pairs + j) * 2 * HEAD_DIM
            o_ref[:, col:col + 2 * HEAD_DIM] = out[j * t:(j + 1) * t].astype(o_ref.dtype)


def attn_prompt(z, sinks, *, batch, seq, n_heads, n_kv):
    t = WINDOW
    nb = seq // t
    hq = n_heads * HEAD_DIM
    hk = n_kv * HEAD_DIM
    kcol = hq // hk
    vcol = kcol + 1
    assert math.frexp(HEAD_DIM ** -0.5)[0] == 0.5

    def cur(b, i, s):
        return b * nb + i

    def prev(b, i, s):
        return b * nb + jnp.maximum(i - 1, 0)

    grid_spec = pltpu.PrefetchScalarGridSpec(
        num_scalar_prefetch=1,
        grid=(batch, nb),
        in_specs=[
            pl.BlockSpec((t, hq), lambda b, i, s: (cur(b, i, s), 0)),
            pl.BlockSpec((t, hk), lambda b, i, s: (prev(b, i, s), kcol)),
            pl.BlockSpec((t, hk), lambda b, i, s: (cur(b, i, s), kcol)),
            pl.BlockSpec((t, hk), lambda b, i, s: (prev(b, i, s), vcol)),
            pl.BlockSpec((t, hk), lambda b, i, s: (cur(b, i, s), vcol)),
        ],
        out_specs=pl.BlockSpec((t, hq), lambda b, i, s: (cur(b, i, s), 0)),
    )
    return pl.pallas_call(
        functools.partial(_attn_prompt_body, n_kv=n_kv, group=n_heads // n_kv, scale=HEAD_DIM ** -0.5),
        grid_spec=grid_spec,
        out_shape=jax.ShapeDtypeStruct((batch * seq, hq), BF16),
        compiler_params=_params(2),
        name="attn_prompt",
    )(sinks, z, z, z, z, z)


def _attn_sample_body(q_ref, kc_ref, vc_ref, kn_ref, vn_ref, sink_ref, o_ref, *, n_kv, group, scale, first_valid):
    n_heads, hd = q_ref.shape[1:]
    hk = n_kv * hd
    head = lax.broadcasted_iota(jnp.int32, (n_heads, hk), 0)
    lane = lax.broadcasted_iota(jnp.int32, (n_heads, hk), 1)
    own = (lane // hd) == (head // group)
    sink = sink_ref[...]
    for b in range(q_ref.shape[0]):
        q = q_ref[b]
        q2 = jnp.concatenate([q, q], axis=1)
        q_wide = jnp.where(own, jnp.concatenate([q2] * (n_kv // 2), axis=1), 0.0).astype(BF16)
        s_c = lax.dot_general(q_wide, kc_ref[b].astype(BF16), (((1,), (1,)), ((), ())),
                              preferred_element_type=F32) * scale
        kb = kn_ref[b].astype(BF16).astype(F32)
        s_n = jnp.sum(q_wide.astype(F32) * kb, axis=-1, keepdims=True) * scale
        if first_valid > 0:
            col = lax.broadcasted_iota(jnp.int32, s_c.shape, 1)
            s_c = jnp.where(col >= first_valid, s_c, NEG_INF)
        m = jnp.maximum(jnp.maximum(jnp.max(s_c, axis=-1, keepdims=True), s_n), sink)
        p_c = jnp.exp(s_c - m)
        p_n = jnp.exp(s_n - m)
        denom = jnp.sum(p_c, axis=-1, keepdims=True) + p_n + jnp.exp(sink - m)
        p_c = p_c / denom
        p_n = p_n / denom
        o_wide = _dot(p_c.astype(BF16), vc_ref[b].astype(BF16))
        o_wide = o_wide + p_n.astype(BF16).astype(F32) * vn_ref[b].astype(BF16).astype(F32)
        o_wide = jnp.where(own, o_wide, 0.0)
        o2 = o_wide[:, 0:2 * hd]
        for c in range(1, n_kv // 2):
            o2 = o2 + o_wide[:, c * 2 * hd:(c + 1) * 2 * hd]
        o_ref[b] = o2[:, 0:hd] + o2[:, hd:2 * hd]


def attn_sample(q, kc, vc, kn, vn, sinks, *, n_kv):
    nseq, n_heads, hd = q.shape
    wb = kc.shape[1]
    hk = kc.shape[2]
    first_valid = max(0, wb - WINDOW, wb - PAST_LEN)
    sb = _tile(nseq, SAMPLE_SEQS)
    return pl.pallas_call(
        functools.partial(_attn_sample_body, n_kv=n_kv, group=n_heads // n_kv, scale=HEAD_DIM ** -0.5,
                          first_valid=first_valid),
        grid=(nseq // sb,),
        in_specs=[
            pl.BlockSpec((sb, n_heads, hd), lambda b: (b, 0, 0)),
            pl.BlockSpec((sb, wb, hk), lambda b: (b, 0, 0)),
            pl.BlockSpec((sb, wb, hk), lambda b: (b, 0, 0)),
            pl.BlockSpec((sb, 1, hk), lambda b: (b, 0, 0)),
            pl.BlockSpec((sb, 1, hk), lambda b: (b, 0, 0)),
            pl.BlockSpec((n_heads, 1), lambda b: (0, 0)),
        ],
        out_specs=pl.BlockSpec((sb, n_heads, hd), lambda b: (b, 0, 0)),
        out_shape=jax.ShapeDtypeStruct((nseq, n_heads, hd), F32),
        compiler_params=_params(1),
        name="attn_sample",
    )(q, kc, vc, kn, vn, sinks)


def _conv_mix_sample_body(gb_ref, cu_ref, s0_ref, s1_ref, w_ref, o_ref):
    w = w_ref[...]
    dw = w[0:1] * s0_ref[...] + w[1:2] * s1_ref[...] + w[2:3] * cu_ref[...]
    o_ref[...] = (gb_ref[...] * dw).astype(o_ref.dtype)


def conv_mix_sample(gb, cu, s0, s1, w):
    return pl.pallas_call(
        _conv_mix_sample_body,
        out_shape=jax.ShapeDtypeStruct(gb.shape, BF16),
        name="conv_mix_sample",
    )(gb, cu, s0, s1, w)


def kernel(x_prompt, x_sample, cache_k, cache_v, state_conv, p_prompt, p_sample, norm_gains, final_norm_gain,
           w_ffn_gate, w_ffn_up, w_ffn_down, w_qkv, b_qkv, attn_sinks, w_o, b_o, w_conv_in, conv_w, w_conv_out,
           w_ple_proj, w_ple_gate):
    batch, seq, d = x_prompt.shape
    nseq, dec_seq, _ = x_sample.shape
    assert dec_seq == 1 and seq % WINDOW == 0 and CONV_WIDTH == conv_w.shape[1]
    depth = norm_gains.shape[0]
    d_ff = w_ffn_gate.shape[-1]
    n_heads = attn_sinks.shape[1]
    qkv_dim = w_qkv.shape[-1]
    n_kv = (qkv_dim // HEAD_DIM - n_heads) // 2
    hq, hk = n_heads * HEAD_DIM, n_kv * HEAD_DIM
    wb = cache_k.shape[2]
    mp, ms = batch * seq, nseq

    f_pad = _cdiv(d_ff, PANEL_FFN) * PANEL_FFN
    tk_d = _tile(f_pad, DOWN_K)
    panel_qkv = _tile(qkv_dim, PANEL_QKV)
    panel_conv = _tile(d, PANEL_CONV)
    panel_ple = _tile(d, PANEL_PLE)
    panel_out = _tile(d, PANEL_OUT)

    def job_gate(i, h):
        return CastJob(w_ffn_gate, (i, h), PANEL_FFN, d)

    def job_up(i, h):
        return CastJob(w_ffn_up, (i, h), PANEL_FFN, d)

    def job_down(i, h):
        return CastJob(w_ffn_down, (i, h), panel_out, f_pad)

    def job_mixer_in(i):
        if i % 2 == 0:
            return CastJob(w_qkv, (i // 2,), panel_qkv, d)
        return CastJob(w_conv_in, (i // 2,), panel_conv, d)

    def job_mixer_out(i):
        if i % 2 == 0:
            return CastJob(w_o, (i // 2,), panel_out, hq)
        return CastJob(w_conv_out, (i // 2,), panel_out, d)

    def job_ple_gate(i):
        return CastJob(w_ple_gate, (i,), panel_ple, d)

    wpp = w_ple_proj.astype(BF16)

    xp = x_prompt.reshape(mp, d)
    xs = x_sample.reshape(ms, d)
    pp = p_prompt.reshape(depth, mp, -1)
    ps = p_sample.reshape(depth, ms, -1)

    wg = cast_panels(job_gate(0, 0))
    wu = cast_panels(job_up(0, 0))

    nkp, nvp, nks, nvs, ncp, ncs = [], [], [], [], [], []
    for i in range(depth):
        g = norm_gains[i][:, None, :]
        last = i == depth - 1
        j = i // 2

        act, act_s, (wd, w_mi, wu_next, wg_next) = swiglu_up(
            xp, xs, g[0], wg, wu, d_ff=d_ff,
            jobs=[job_down(i, 0), job_mixer_in(i), job_up(i, 1), job_gate(i, 1)])
        xp, xs, (w_mo,) = matmul_residual(act, act_s, wd, xp, xs, scale=0.5, tk=tk_d, k_valid=d_ff,
                                          jobs=[job_mixer_out(i)])

        if i % 2 == 0:
            zp, zs, _ = norm_linear(xp, xs, g[1], w_mi, b_qkv[j][None, :])
            op = attn_prompt(zp, attn_sinks[j], batch=batch, seq=seq, n_heads=n_heads, n_kv=n_kv)
            kn = zs[:, hq:hq + hk]
            vn = zs[:, hq + hk:]
            os_ = attn_sample(zs[:, :hq].reshape(ms, n_heads, HEAD_DIM),
                              cache_k[j].reshape(ms, wb, hk), cache_v[j].reshape(ms, wb, hk),
                              kn[:, None, :], vn[:, None, :], attn_sinks[j][:, None],
                              n_kv=n_kv)
            os_ = os_.reshape(ms, hq).astype(BF16)
            xp, xs, _ = matmul_residual(op, os_, w_mo, xp, xs, b_o[j][None, :], scale=1.0, tk=hq)
            zp3 = zp.reshape(batch, seq, qkv_dim)
            wp_ = min(WINDOW, seq)
            nkp.append(zp3[:, seq - wp_:, hq:hq + hk].reshape(batch, wp_, n_kv, HEAD_DIM))
            nvp.append(zp3[:, seq - wp_:, hq + hk:].reshape(batch, wp_, n_kv, HEAD_DIM))
            kk = jnp.concatenate([cache_k[j], kn.reshape(ms, 1, n_kv, HEAD_DIM)], axis=1)
            vv = jnp.concatenate([cache_v[j], vn.reshape(ms, 1, n_kv, HEAD_DIM)], axis=1)
            nks.append(kk[:, -wb:])
            nvs.append(vv[:, -wb:])
        else:
            (tp, cup, _), (_, cus, gbs), _ = conv_mixer_in(xp, xs, g[1], w_mi, conv_w[j], seq=seq)
            st = state_conv[j]
            ts = conv_mix_sample(gbs, cus, st[:, 0], st[:, 1], conv_w[j])
            xp, xs, _ = matmul_residual(tp, ts, w_mo, xp, xs, scale=1.0, tk=d)
            ncp.append(cup.reshape(batch, seq, d)[:, seq - (CONV_WIDTH - 1):])
            ncs.append(jnp.concatenate([st, cus[:, None, :]], axis=1)[:, -(CONV_WIDTH - 1):])

        next_up = [] if last else [job_gate(i + 1, 0), job_up(i + 1, 0)]
        act, act_s, (wd, w_pg, *wg_wu) = swiglu_up(xp, xs, g[2], wg_next, wu_next, d_ff=d_ff,
                                                   jobs=[job_down(i, 1), job_ple_gate(i)] + next_up)
        xp, xs, _ = matmul_residual(act, act_s, wd, xp, xs, scale=0.5, tk=tk_d, k_valid=d_ff)
        if not last:
            wg, wu = wg_wu

        xp, xs, _ = ple(xp, xs, g[3], w_pg, pp[i], ps[i], wpp[i])

    gf = final_norm_gain[None, :]
    y_prompt = rmsnorm(xp, gf, tm=_tile(mp, 256)).reshape(batch, seq, d)
    y_sample = rmsnorm(xs, gf, tm=ms).reshape(nseq, dec_seq, d)
    return (y_prompt, y_sample, jnp.stack(nkp), jnp.stack(nvp), jnp.stack(nks), jnp.stack(nvs),
            jnp.stack(ncp), jnp.stack(ncs))
```

```python
import functools
import math
from typing import NamedTuple

import jax
import jax.numpy as jnp
from jax import lax
from jax.experimental import pallas as pl
from jax.experimental.pallas import tpu as pltpu

NORM_EPS = 1e-6
NEG_INF = -1e30
HEAD_DIM = 64
WINDOW = 128
CONV_WIDTH = 3
PAST_LEN = 8192

V7X_VMEM_LIMIT_BYTES = 58 * 1024 * 1024
BF16_SUBLANES = 16

GROUP_ROWS = 1024
WIDE_GROUP_ROWS = 2048
ROW_TILE = 512
NORM_ROWS = 256
PANEL_FFN = 512
PANEL_QKV = 512
PANEL_CONV = 256
PANEL_PLE = 512
PANEL_OUT = 1024
DOWN_ROWS = 1024
DOWN_K = 2816
SAMPLE_SEQS = 8

BF16 = jnp.bfloat16
F32 = jnp.float32


def _params(n_axes):
    return pltpu.CompilerParams(
        dimension_semantics=("arbitrary",) * n_axes,
        vmem_limit_bytes=V7X_VMEM_LIMIT_BYTES,
    )


def _tile(dim, target):
    if dim <= target:
        return dim
    t = target
    while dim % t:
        t //= 2
    return t


def _cdiv(a, b):
    return -(-a // b)


def _dot(a, b):
    return jnp.dot(a, b, preferred_element_type=F32)


class CastJob(NamedTuple):
    src: jax.Array
    lead: tuple
    tn: int
    rows_out: int


class _JobPlan(NamedTuple):
    rows: int
    cols: int
    tn: int
    panels: int
    rb: int
    nb_src: int
    nb_dst: int
    rows_out: int


def _plan_job(job, n_steps):
    rows, cols = job.src.shape[-2:]
    rb = BF16_SUBLANES
    while _cdiv(job.rows_out, rb) > n_steps:
        rb *= 2
    assert rows % rb == 0 and job.rows_out % rb == 0, (rows, job.rows_out, rb)
    return _JobPlan(rows, cols, job.tn, _cdiv(cols, job.tn), rb, rows // rb, job.rows_out // rb, job.rows_out)


def _job_specs(job, plan, step_of):
    n_lead = len(job.lead)

    def src_idx(*g):
        return (*job.lead, jnp.minimum(step_of(*g), plan.nb_src - 1), 0)

    def dst_idx(*g):
        return (0, jnp.minimum(step_of(*g), plan.nb_dst - 1), 0)

    src_spec = pl.BlockSpec((None,) * n_lead + (plan.rb, plan.cols), src_idx)
    dst_spec = pl.BlockSpec((plan.panels, plan.rb, plan.tn), dst_idx)
    dst_shape = jax.ShapeDtypeStruct((plan.panels, plan.rows_out, plan.tn), BF16)
    return src_spec, dst_spec, dst_shape


def _run_job(plan, src_ref, dst_ref, step):
    @pl.when(step < plan.nb_dst)
    def _():
        v = src_ref[...]
        if plan.rows_out > plan.rows:
            row = step * plan.rb + lax.broadcasted_iota(jnp.int32, (plan.rb, 1), 0)
            v = jnp.where(row < plan.rows, v, 0.0)
        for p in range(plan.panels):
            lo = p * plan.tn
            width = min(plan.tn, plan.cols - lo)
            dst_ref[p, :, 0:width] = v[:, lo:lo + width].astype(BF16)
            if width < plan.tn:
                dst_ref[p, :, width:plan.tn] = jnp.zeros((plan.rb, plan.tn - width), BF16)


def _call_with_jobs(body, *, name, grid, in_specs, out_specs, out_shape, scratch_shapes, args, jobs):
    n_steps = 1
    for gdim in grid:
        n_steps *= gdim

    def step_of(*g):
        s = g[0]
        for gdim, gi in zip(grid[1:], g[1:]):
            s = s * gdim + gi
        return s

    plans = [_plan_job(j, n_steps) for j in jobs]
    specs = [_job_specs(j, p, step_of) for j, p in zip(jobs, plans)]
    n_in, n_out, n_job = len(in_specs), len(out_specs), len(jobs)

    def kern(*refs):
        ins = refs[:n_in]
        job_src = refs[n_in:n_in + n_job]
        outs = refs[n_in + n_job:n_in + n_job + n_out]
        job_dst = refs[n_in + n_job + n_out:n_in + 2 * n_job + n_out]
        scratch = refs[n_in + 2 * n_job + n_out:]

        def run_jobs():
            if n_job:
                step = step_of(*[pl.program_id(a) for a in range(len(grid))])
                for plan, s_ref, d_ref in zip(plans, job_src, job_dst):
                    _run_job(plan, s_ref, d_ref, step)

        body(run_jobs, *ins, *outs, *scratch)

    res = pl.pallas_call(
        kern,
        grid=grid,
        in_specs=list(in_specs) + [s[0] for s in specs],
        out_specs=list(out_specs) + [s[1] for s in specs],
        out_shape=list(out_shape) + [s[2] for s in specs],
        scratch_shapes=scratch_shapes,
        compiler_params=_params(len(grid)),
        name=name,
    )(*args, *[j.src for j in jobs])
    return list(res[:n_out]), list(res[n_out:])


def cast_panels(job):
    n_steps = _cdiv(job.rows_out, 8 * BF16_SUBLANES)
    _, copies = _call_with_jobs(lambda run_jobs: run_jobs(), name="cast_panels", grid=(n_steps,), in_specs=[],
                                out_specs=[], out_shape=[], scratch_shapes=[], args=[], jobs=[job])
    return copies[0]


def _norm_rows_into(h_ref, row0, x_ref, g_ref):
    g = g_ref[...]
    chunk = BF16_SUBLANES

    def body(c, carry):
        r = pl.multiple_of(c * chunk, chunk)
        x = x_ref[pl.ds(r, chunk), :]
        y = x * lax.rsqrt(jnp.mean(x * x, axis=-1, keepdims=True) + NORM_EPS) * g
        h_ref[pl.ds(row0 + r, chunk), :] = y.astype(BF16)
        return carry

    n_chunks = x_ref.shape[0] // chunk
    lax.fori_loop(0, n_chunks, body, 0, unroll=min(n_chunks, 8))


class _Extra(NamedTuple):
    array: jax.Array
    cols: int
    by_panel: bool
    by_row: bool
    sample: object


class _TileCtx(NamedTuple):
    s: int
    n_sub: int
    tm: int
    group: object
    panel: object
    scratch: tuple


def _norm_matmul(name, epilogue, x, g, panels, extras, outs, *, tn, n_panels, x_sample, last_cols=None,
                 group_rows=GROUP_ROWS, scratch=(), jobs=()):
    m, d = x.shape
    last_cols = tn if last_cols is None else last_cols
    assert last_cols == tn or not any(e.by_panel for e in extras)
    ms = x_sample.shape[0]
    group = _tile(m, group_rows)
    tm = _tile(group, ROW_TILE)
    xb = _tile(group, NORM_ROWS)
    pre = group // xb
    n_groups = m // group
    last = n_groups - 1
    grid = (n_groups, pre + n_panels)
    n_p, n_e, n_o = len(panels), len(extras), len(outs)
    row_extras = [k for k, e in enumerate(extras) if e.by_row]

    def panel_of(n):
        return jnp.maximum(n - pre, 0)

    def sample_panel_of(i, n):
        return jnp.where(i == last, panel_of(n), 0)

    in_specs = [
        pl.BlockSpec((xb, d), lambda i, n: (i * pre + jnp.minimum(n, pre - 1), 0)),
        pl.BlockSpec((1, d), lambda i, n: (0, 0)),
    ]
    args = [x, g]
    for w, off in panels:
        in_specs.append(pl.BlockSpec((None, d, tn), functools.partial(lambda i, n, o: (panel_of(n) + o, 0, 0), o=off)))
        args.append(w)
    for e in extras:
        rows = group if e.by_row else e.array.shape[0]
        in_specs.append(pl.BlockSpec(
            (rows, e.cols),
            functools.partial(lambda i, n, r, c: (i if r else 0, panel_of(n) if c else 0), r=e.by_row, c=e.by_panel)))
        args.append(e.array)
    in_specs.append(pl.BlockSpec((ms, d), lambda i, n: (0, 0)))
    args.append(x_sample)
    for k in row_extras:
        e = extras[k]
        in_specs.append(pl.BlockSpec(
            (ms, e.cols), functools.partial(lambda i, n, c: (0, sample_panel_of(i, n) if c else 0), c=e.by_panel)))
        args.append(e.sample)
    n_in = len(in_specs)

    out_specs = [pl.BlockSpec((group, tn), lambda i, n: (i, panel_of(n))) for _ in outs]
    out_specs += [pl.BlockSpec((ms, tn), lambda i, n: (0, sample_panel_of(i, n))) for _ in outs]
    out_shape = [jax.ShapeDtypeStruct((m, cols), dt) for cols, dt in outs]
    out_shape += [jax.ShapeDtypeStruct((ms, cols), dt) for cols, dt in outs]

    def body(run_jobs, *refs):
        x_ref, g_ref = refs[:2]
        panel_refs = refs[2:2 + n_p]
        extra_refs = refs[2 + n_p:2 + n_p + n_e]
        xs_ref = refs[2 + n_p + n_e]
        extra_s_refs = dict(zip(row_extras, refs[3 + n_p + n_e:n_in]))
        out_refs = refs[n_in:n_in + n_o]
        out_s_refs = refs[n_in + n_o:n_in + 2 * n_o]
        h_ref = refs[n_in + 2 * n_o]
        extra_scratch = tuple(refs[n_in + 2 * n_o + 1:])
        i, n = pl.program_id(0), pl.program_id(1)

        @pl.when(n < pre)
        def _():
            _norm_rows_into(h_ref, pl.multiple_of(n * xb, xb), x_ref, g_ref)
            run_jobs()

        @pl.when((n == 0) & (i == last))
        def _():
            _norm_rows_into(h_ref, group, xs_ref, g_ref)

        def store(ref, rows, v):
            width = v.shape[1]
            ref[rows, 0:width] = v
            if width < tn:
                ref[rows, width:tn] = jnp.zeros((v.shape[0], tn - width), v.dtype)

        def panel_getter(cols):
            return lambda k: panel_refs[k][...] if cols == tn else panel_refs[k][:, 0:cols]

        def panel_step(cols, with_sample):
            n_sub = group // tm
            for s in range(n_sub):
                rows = slice(s * tm, (s + 1) * tm)
                joined = with_sample and s == n_sub - 1

                def get_extra(k, rows=rows, joined=joined):
                    if not extras[k].by_row:
                        return extra_refs[k][...]
                    v = extra_refs[k][rows, :]
                    return jnp.concatenate([v, extra_s_refs[k][...]], axis=0) if joined else v

                def put_out(k, v, rows=rows, joined=joined):
                    store(out_refs[k], rows, v[0:tm])
                    if joined:
                        store(out_s_refs[k], slice(None), v[tm:tm + ms])

                h_rows = slice(s * tm, group + ms) if joined else rows
                ctx = _TileCtx(s, n_sub, tm, i, n - pre, extra_scratch)
                epilogue(h_ref[h_rows, :], panel_getter(cols), get_extra, put_out, ctx)
            run_jobs()

        last_panel = pre + n_panels - 1
        variants = [(n >= pre, tn)] if last_cols == tn else [((n >= pre) & (n < last_panel), tn),
                                                            (n == last_panel, last_cols)]
        for cond, cols in variants:
            pl.when(cond & (i < last))(functools.partial(panel_step, cols, False))
            pl.when(cond & (i == last))(functools.partial(panel_step, cols, True))

    outs_all, copies = _call_with_jobs(
        body, name=name, grid=grid, in_specs=in_specs, out_specs=out_specs, out_shape=out_shape,
        scratch_shapes=[pltpu.VMEM((group + ms, d), BF16)] + [f(tm, tn) for f in scratch], args=args,
        jobs=list(jobs))
    return outs_all[:n_o], outs_all[n_o:], copies


def _swiglu_epilogue(h, panel, get_extra, put_out, ctx):
    gate = _dot(h, panel(0))
    up = _dot(h, panel(1))
    put_out(0, (gate * jax.nn.sigmoid(gate) * up).astype(BF16))


def swiglu_up(x, x_sample, g, wg, wu, *, d_ff, jobs=()):
    n_panels, _, tn = wg.shape
    outs, outs_s, copies = _norm_matmul(
        "swiglu_up", _swiglu_epilogue, x, g, [(wg, 0), (wu, 0)], [], [(n_panels * tn, BF16)],
        tn=tn, n_panels=n_panels, x_sample=x_sample, last_cols=d_ff - (n_panels - 1) * tn, jobs=jobs)
    return outs[0], outs_s[0], copies


def _linear_epilogue(h, panel, get_extra, put_out, ctx):
    put_out(0, _dot(h, panel(0)) + get_extra(0))


def norm_linear(x, x_sample, g, w, b, *, jobs=()):
    n_panels, _, tn = w.shape
    outs, outs_s, copies = _norm_matmul(
        "norm_linear", _linear_epilogue, x, g, [(w, 0)], [_Extra(b, tn, True, False, None)],
        [(n_panels * tn, F32)], tn=tn, n_panels=n_panels, x_sample=x_sample, group_rows=WIDE_GROUP_ROWS,
        jobs=jobs)
    return outs[0], outs_s[0], copies


def _conv_mixer_epilogue(h, panel, get_extra, put_out, ctx, *, groups_per_seq):
    pad_ref, carry_ref = ctx.scratch
    tm = ctx.tm
    gb = _dot(h, panel(0))
    cu = _dot(h, panel(1)) * _dot(h, panel(2))
    w = get_extra(0)
    if ctx.s > 0:
        halo = pad_ref[tm:tm + 8, :]
    elif groups_per_seq == 1:
        halo = jnp.zeros((8, cu.shape[1]), F32)
    else:
        halo = jnp.where(ctx.group % groups_per_seq == 0, 0.0, carry_ref[ctx.panel])
    pad_ref[0:8, :] = halo
    pad_ref[8:8 + tm, :] = cu[0:tm]
    dw = w[0:1] * pad_ref[6:6 + tm, :] + w[1:2] * pad_ref[7:7 + tm, :] + w[2:3] * cu[0:tm]
    if ctx.s == ctx.n_sub - 1 and groups_per_seq > 1:
        carry_ref[ctx.panel] = cu[tm - 8:tm]
    mixed = (gb[0:tm] * dw).astype(BF16)
    extra_rows = h.shape[0] - tm
    if extra_rows:
        mixed = jnp.concatenate([mixed, jnp.zeros((extra_rows, mixed.shape[1]), BF16)], axis=0)
    put_out(0, mixed)
    put_out(1, cu)
    put_out(2, gb)


def conv_mixer_in(x, x_sample, g, w_in, conv_w, *, seq, jobs=()):
    m, d = x.shape
    tn = w_in.shape[2]
    n_panels = d // tn
    group = _tile(m, WIDE_GROUP_ROWS)
    assert seq % group == 0 and CONV_WIDTH - 1 <= 8
    outs, outs_s, copies = _norm_matmul(
        "conv_mixer_in", functools.partial(_conv_mixer_epilogue, groups_per_seq=seq // group), x, g,
        [(w_in, 0), (w_in, n_panels), (w_in, 2 * n_panels)], [_Extra(conv_w, tn, True, False, None)],
        [(d, BF16), (d, F32), (d, F32)], tn=tn, n_panels=n_panels, x_sample=x_sample,
        group_rows=WIDE_GROUP_ROWS,
        scratch=[lambda tm, tn: pltpu.VMEM((tm + 16, tn), F32),
                 lambda tm, tn: pltpu.VMEM((n_panels, 8, tn), F32)],
        jobs=jobs)
    return outs, outs_s, copies


def _ple_epilogue(h, panel, get_extra, put_out, ctx):
    gate = jax.nn.sigmoid(_dot(h, panel(0)))
    proj = _dot(get_extra(1).astype(BF16), get_extra(2))
    put_out(0, get_extra(0) + gate * proj)


def ple(x, x_sample, g, w_gate, p, p_sample, w_proj, *, jobs=()):
    d = x.shape[1]
    pd = p.shape[1]
    n_panels, _, tn = w_gate.shape
    extras = [
        _Extra(x, tn, True, True, x_sample),
        _Extra(p, pd, False, True, p_sample),
        _Extra(w_proj, tn, True, False, None),
    ]
    outs, outs_s, copies = _norm_matmul(
        "ple", _ple_epilogue, x, g, [(w_gate, 0)], extras, [(d, F32)],
        tn=tn, n_panels=n_panels, x_sample=x_sample, group_rows=WIDE_GROUP_ROWS, jobs=jobs)
    return outs[0], outs_s[0], copies


class _RowSet(NamedTuple):
    a: object
    x: object
    o: object
    acc: object


def _residual_step(row_sets, w_ref, b_ref, *, scale, n_k, k, last_k_cols, then):
    tk, tn = w_ref.shape
    halves = [slice(c * (tn // 2), (c + 1) * (tn // 2)) for c in range(2)] if tn % 256 == 0 else [slice(0, tn)]

    def products(cols, k_cols):
        parts = [rs.a[...] if k_cols == tk else rs.a[:, 0:k_cols] for rs in row_sets]
        a = parts[0] if len(parts) == 1 else jnp.concatenate(parts, axis=0)
        y = _dot(a, w_ref[:, cols] if k_cols == tk else w_ref[0:k_cols, cols])
        out, r0 = [], 0
        for p in parts:
            out.append(y[r0:r0 + p.shape[0]])
            r0 += p.shape[0]
        return out

    def finish(rs, y, cols):
        if b_ref is not None:
            y = y + b_ref[:, cols]
        rs.o[:, cols] = rs.x[:, cols] + scale * y

    if n_k == 1:
        for cols in halves:
            for rs, y in zip(row_sets, products(cols, last_k_cols)):
                finish(rs, y, cols)
        then()
        return

    @pl.when(k < n_k - 1)
    def _():
        for cols in halves:
            for rs, y in zip(row_sets, products(cols, tk)):
                rs.acc[:, cols] += y
        then()

    @pl.when(k == n_k - 1)
    def _():
        for cols in halves:
            for rs, y in zip(row_sets, products(cols, last_k_cols)):
                acc = rs.acc[:, cols] + y
                rs.acc[:, cols] = jnp.zeros_like(acc)
                finish(rs, acc, cols)
        then()


def _matmul_residual_body(run_jobs, *refs, scale, has_bias, n_k, last_tile, last_k_cols):
    a_ref, w_ref, x_ref, as_ref, xs_ref = refs[:5]
    b_ref = refs[5] if has_bias else None
    o_ref, os_ref = refs[5 + has_bias:7 + has_bias]
    acc_ref, accs_ref = refs[7 + has_bias:9 + has_bias] if n_k > 1 else (None, None)
    i, j, k = pl.program_id(0), pl.program_id(1), pl.program_id(2)
    prompt = _RowSet(a_ref, x_ref, o_ref, acc_ref)
    sample = _RowSet(as_ref, xs_ref, os_ref, accs_ref)

    if n_k > 1:
        @pl.when((i == 0) & (j == 0) & (k == 0))
        def _():
            acc_ref[...] = jnp.zeros_like(acc_ref)

        @pl.when((i == last_tile) & (j == 0) & (k == 0))
        def _():
            accs_ref[...] = jnp.zeros_like(accs_ref)

    step = functools.partial(_residual_step, w_ref=w_ref, b_ref=b_ref, scale=scale, n_k=n_k, k=k,
                             last_k_cols=last_k_cols, then=run_jobs)
    pl.when(i < last_tile)(lambda: step([prompt]))
    pl.when(i == last_tile)(lambda: step([prompt, sample]))


def matmul_residual(a, a_sample, w, x, x_sample, b=None, *, scale, tk, k_valid=None, jobs=()):
    m, kdim = a.shape
    ms = a_sample.shape[0]
    n_panels, kw, tn = w.shape
    tm = _tile(m, DOWN_ROWS)
    assert kw == kdim and kdim % tk == 0 and m % tm == 0
    n_k = kdim // tk
    last_tile = m // tm - 1
    last_k_cols = tk if k_valid is None else k_valid - (n_k - 1) * tk
    assert 0 < last_k_cols <= tk and last_k_cols % 128 == 0

    def sample_col(i, j):
        return jnp.where(i == last_tile, j, 0)

    in_specs = [
        pl.BlockSpec((tm, tk), lambda i, j, k: (i, k)),
        pl.BlockSpec((None, tk, tn), lambda i, j, k: (j, k, 0)),
        pl.BlockSpec((tm, tn), lambda i, j, k: (i, j)),
        pl.BlockSpec((ms, tk), lambda i, j, k: (0, k)),
        pl.BlockSpec((ms, tn), lambda i, j, k: (0, sample_col(i, j))),
    ]
    args = [a, w, x, a_sample, x_sample]
    if b is not None:
        in_specs.append(pl.BlockSpec((1, tn), lambda i, j, k: (0, j)))
        args.append(b)
    scratch = [pltpu.VMEM((tm, tn), F32), pltpu.VMEM((ms, tn), F32)] if n_k > 1 else []
    outs, copies = _call_with_jobs(
        functools.partial(_matmul_residual_body, scale=scale, has_bias=b is not None, n_k=n_k, last_tile=last_tile,
                          last_k_cols=last_k_cols),
        name="matmul_residual", grid=(m // tm, n_panels, n_k), in_specs=in_specs,
        out_specs=[pl.BlockSpec((tm, tn), lambda i, j, k: (i, j)),
                   pl.BlockSpec((ms, tn), lambda i, j, k: (0, sample_col(i, j)))],
        out_shape=[jax.ShapeDtypeStruct((m, n_panels * tn), F32), jax.ShapeDtypeStruct((ms, n_panels * tn), F32)],
        scratch_shapes=scratch, args=args, jobs=list(jobs))
    return outs[0], outs[1], copies


def _rmsnorm_kernel(x_ref, g_ref, o_ref):
    x = x_ref[...]
    o_ref[...] = x * lax.rsqrt(jnp.mean(x * x, axis=-1, keepdims=True) + NORM_EPS) * g_ref[...]


def rmsnorm(x, g, *, tm):
    m, d = x.shape
    return pl.pallas_call(
        _rmsnorm_kernel,
        grid=(m // tm,),
        in_specs=[pl.BlockSpec((tm, d), lambda i: (i, 0)), pl.BlockSpec((1, d), lambda i: (0, 0))],
        out_specs=pl.BlockSpec((tm, d), lambda i: (i, 0)),
        out_shape=jax.ShapeDtypeStruct((m, d), F32),
        compiler_params=_params(1),
        name="rmsnorm",
    )(x, g)


def _attn_prompt_body(sink_ref, q_ref, kp_ref, kc_ref, vp_ref, vc_ref, o_ref, *, n_kv, group, scale):
    blk = pl.program_id(1)
    t = WINDOW
    pairs = group // 2
    q = (q_ref[...] * scale).astype(BF16)
    k_all = jnp.concatenate([kp_ref[...], kc_ref[...]], axis=0)
    v_all = jnp.concatenate([vp_ref[...], vc_ref[...]], axis=0)

    r = lax.broadcasted_iota(jnp.int32, (t, 2 * t), 0)
    c = lax.broadcasted_iota(jnp.int32, (t, 2 * t), 1)
    first_col = jnp.where(blk > 0, r, t)
    valid = (c >= first_col) & (c <= r + t)
    lane = lax.broadcasted_iota(jnp.int32, (2 * t, 2 * HEAD_DIM), 1)
    low = lane < HEAD_DIM

    for n in range(n_kv):
        slab = slice((n // 2) * 2 * HEAD_DIM, (n // 2 + 1) * 2 * HEAD_DIM)
        k2 = k_all[:, slab]
        v2 = v_all[:, slab]
        if n % 2 == 0:
            k_lo = jnp.where(low, k2, 0.0)
            v_lo = jnp.where(low, v2, 0.0)
            k_hi = pltpu.roll(k_lo, HEAD_DIM, 1)
            v_hi = pltpu.roll(v_lo, HEAD_DIM, 1)
        else:
            k_hi = jnp.where(low, 0.0, k2)
            v_hi = jnp.where(low, 0.0, v2)
            k_lo = pltpu.roll(k_hi, HEAD_DIM, 1)
            v_lo = pltpu.roll(v_hi, HEAD_DIM, 1)
        k_halves = (k_lo.astype(BF16), k_hi.astype(BF16))
        v_halves = (v_lo.astype(BF16), v_hi.astype(BF16))

        q_slabs = [q[:, (n * pairs + j) * 2 * HEAD_DIM:(n * pairs + j + 1) * 2 * HEAD_DIM] for j in range(pairs)]
        q4 = jnp.concatenate(q_slabs, axis=0)
        out = None
        for half in range(2):
            s = lax.dot_general(q4, k_halves[half], (((1,), (1,)), ((), ())),
                                preferred_element_type=F32)
            probs = []
            for j in range(pairs):
                sink = sink_ref[n * group + 2 * j + half]
                sj = jnp.where(valid, s[j * t:(j + 1) * t], NEG_INF)
                m = jnp.maximum(jnp.max(sj, axis=-1, keepdims=True), sink)
                pj = jnp.exp(sj - m)
                pj = pj / (jnp.sum(pj, axis=-1, keepdims=True) + jnp.exp(sink - m))
                probs.append(pj.astype(BF16))
            part = _dot(jnp.concatenate(probs, axis=0), v_halves[half])
            out = part if out is None else out + part
        for j in range(pairs):
            col = (n * pairs + j) * 2 * HEAD_DIM
            o_ref[:, col:col + 2 * HEAD_DIM] = out[j * t:(j + 1) * t].astype(o_ref.dtype)


def attn_prompt(z, sinks, *, batch, seq, n_heads, n_kv):
    t = WINDOW
    nb = seq // t
    hq = n_heads * HEAD_DIM
    hk = n_kv * HEAD_DIM
    kcol = hq // hk
    vcol = kcol + 1
    assert math.frexp(HEAD_DIM ** -0.5)[0] == 0.5

    def cur(b, i, s):
        return b * nb + i

    def prev(b, i, s):
        return b * nb + jnp.maximum(i - 1, 0)

    grid_spec = pltpu.PrefetchScalarGridSpec(
        num_scalar_prefetch=1,
        grid=(batch, nb),
        in_specs=[
            pl.BlockSpec((t, hq), lambda b, i, s: (cur(b, i, s), 0)),
            pl.BlockSpec((t, hk), lambda b, i, s: (prev(b, i, s), kcol)),
            pl.BlockSpec((t, hk), lambda b, i, s: (cur(b, i, s), kcol)),
            pl.BlockSpec((t, hk), lambda b, i, s: (prev(b, i, s), vcol)),
            pl.BlockSpec((t, hk), lambda b, i, s: (cur(b, i, s), vcol)),
        ],
        out_specs=pl.BlockSpec((t, hq), lambda b, i, s: (cur(b, i, s), 0)),
    )
    return pl.pallas_call(
        functools.partial(_attn_prompt_body, n_kv=n_kv, group=n_heads // n_kv, scale=HEAD_DIM ** -0.5),
        grid_spec=grid_spec,
        out_shape=jax.ShapeDtypeStruct((batch * seq, hq), BF16),
        compiler_params=_params(2),
        name="attn_prompt",
    )(sinks, z, z, z, z, z)


def _attn_sample_body(q_ref, kc_ref, vc_ref, kn_ref, vn_ref, sink_ref, o_ref, *, n_kv, group, scale, first_valid):
    n_heads, hd = q_ref.shape[1:]
    hk = n_kv * hd
    head = lax.broadcasted_iota(jnp.int32, (n_heads, hk), 0)
    lane = lax.broadcasted_iota(jnp.int32, (n_heads, hk), 1)
    own = (lane // hd) == (head // group)
    sink = sink_ref[...]
    for b in range(q_ref.shape[0]):
        q = q_ref[b]
        q2 = jnp.concatenate([q, q], axis=1)
        q_wide = jnp.where(own, jnp.concatenate([q2] * (n_kv // 2), axis=1), 0.0).astype(BF16)
        s_c = lax.dot_general(q_wide, kc_ref[b].astype(BF16), (((1,), (1,)), ((), ())),
                              preferred_element_type=F32) * scale
        kb = kn_ref[b].astype(BF16).astype(F32)
        s_n = jnp.sum(q_wide.astype(F32) * kb, axis=-1, keepdims=True) * scale
        if first_valid > 0:
            col = lax.broadcasted_iota(jnp.int32, s_c.shape, 1)
            s_c = jnp.where(col >= first_valid, s_c, NEG_INF)
        m = jnp.maximum(jnp.maximum(jnp.max(s_c, axis=-1, keepdims=True), s_n), sink)
        p_c = jnp.exp(s_c - m)
        p_n = jnp.exp(s_n - m)
        denom = jnp.sum(p_c, axis=-1, keepdims=True) + p_n + jnp.exp(sink - m)
        p_c = p_c / denom
        p_n = p_n / denom
        o_wide = _dot(p_c.astype(BF16), vc_ref[b].astype(BF16))
        o_wide = o_wide + p_n.astype(BF16).astype(F32) * vn_ref[b].astype(BF16).astype(F32)
        o_wide = jnp.where(own, o_wide, 0.0)
        o2 = o_wide[:, 0:2 * hd]
        for c in range(1, n_kv // 2):
            o2 = o2 + o_wide[:, c * 2 * hd:(c + 1) * 2 * hd]
        o_ref[b] = o2[:, 0:hd] + o2[:, hd:2 * hd]


def attn_sample(q, kc, vc, kn, vn, sinks, *, n_kv):
    nseq, n_heads, hd = q.shape
    wb = kc.shape[1]
    hk = kc.shape[2]
    first_valid = max(0, wb - WINDOW, wb - PAST_LEN)
    sb = _tile(nseq, SAMPLE_SEQS)
    return pl.pallas_call(
        functools.partial(_attn_sample_body, n_kv=n_kv, group=n_heads // n_kv, scale=HEAD_DIM ** -0.5,
                          first_valid=first_valid),
        grid=(nseq // sb,),
        in_specs=[
            pl.BlockSpec((sb, n_heads, hd), lambda b: (b, 0, 0)),
            pl.BlockSpec((sb, wb, hk), lambda b: (b, 0, 0)),
            pl.BlockSpec((sb, wb, hk), lambda b: (b, 0, 0)),
            pl.BlockSpec((sb, 1, hk), lambda b: (b, 0, 0)),
            pl.BlockSpec((sb, 1, hk), lambda b: (b, 0, 0)),
            pl.BlockSpec((n_heads, 1), lambda b: (0, 0)),
        ],
        out_specs=pl.BlockSpec((sb, n_heads, hd), lambda b: (b, 0, 0)),
        out_shape=jax.ShapeDtypeStruct((nseq, n_heads, hd), F32),
        compiler_params=_params(1),
        name="attn_sample",
    )(q, kc, vc, kn, vn, sinks)


def _conv_mix_sample_body(gb_ref, cu_ref, s0_ref, s1_ref, w_ref, o_ref):
    w = w_ref[...]
    dw = w[0:1] * s0_ref[...] + w[1:2] * s1_ref[...] + w[2:3] * cu_ref[...]
    o_ref[...] = (gb_ref[...] * dw).astype(o_ref.dtype)


def conv_mix_sample(gb, cu, s0, s1, w):
    return pl.pallas_call(
        _conv_mix_sample_body,
        out_shape=jax.ShapeDtypeStruct(gb.shape, BF16),
        name="conv_mix_sample",
    )(gb, cu, s0, s1, w)


def kernel(x_prompt, x_sample, cache_k, cache_v, state_conv, p_prompt, p_sample, norm_gains, final_norm_gain,
           w_ffn_gate, w_ffn_up, w_ffn_down, w_qkv, b_qkv, attn_sinks, w_o, b_o, w_conv_in, conv_w, w_conv_out,
           w_ple_proj, w_ple_gate):
    batch, seq, d = x_prompt.shape
    nseq, dec_seq, _ = x_sample.shape
    assert dec_seq == 1 and seq % WINDOW == 0 and CONV_WIDTH == conv_w.shape[1]
    depth = norm_gains.shape[0]
    d_ff = w_ffn_gate.shape[-1]
    n_heads = attn_sinks.shape[1]
    qkv_dim = w_qkv.shape[-1]
    n_kv = (qkv_dim // HEAD_DIM - n_heads) // 2
    hq, hk = n_heads * HEAD_DIM, n_kv * HEAD_DIM
    wb = cache_k.shape[2]
    mp, ms = batch * seq, nseq

    f_pad = _cdiv(d_ff, PANEL_FFN) * PANEL_FFN
    tk_d = _tile(f_pad, DOWN_K)
    panel_qkv = _tile(qkv_dim, PANEL_QKV)
    panel_conv = _tile(d, PANEL_CONV)
    panel_ple = _tile(d, PANEL_PLE)
    panel_out = _tile(d, PANEL_OUT)

    def job_gate(i, h):
        return CastJob(w_ffn_gate, (i, h), PANEL_FFN, d)

    def job_up(i, h):
        return CastJob(w_ffn_up, (i, h), PANEL_FFN, d)

    def job_down(i, h):
        return CastJob(w_ffn_down, (i, h), panel_out, f_pad)

    def job_mixer_in(i):
        if i % 2 == 0:
            return CastJob(w_qkv, (i // 2,), panel_qkv, d)
        return CastJob(w_conv_in, (i // 2,), panel_conv, d)

    def job_mixer_out(i):
        if i % 2 == 0:
            return CastJob(w_o, (i // 2,), panel_out, hq)
        return CastJob(w_conv_out, (i // 2,), panel_out, d)

    def job_ple_gate(i):
        return CastJob(w_ple_gate, (i,), panel_ple, d)

    wpp = w_ple_proj.astype(BF16)

    xp = x_prompt.reshape(mp, d)
    xs = x_sample.reshape(ms, d)
    pp = p_prompt.reshape(depth, mp, -1)
    ps = p_sample.reshape(depth, ms, -1)

    wg = cast_panels(job_gate(0, 0))
    wu = cast_panels(job_up(0, 0))

    nkp, nvp, nks, nvs, ncp, ncs = [], [], [], [], [], []
    for i in range(depth):
        g = norm_gains[i][:, None, :]
        last = i == depth - 1
        j = i // 2

        act, act_s, (wd, w_mi, wu_next, wg_next) = swiglu_up(
            xp, xs, g[0], wg, wu, d_ff=d_ff,
            jobs=[job_down(i, 0), job_mixer_in(i), job_up(i, 1), job_gate(i, 1)])
        xp, xs, (w_mo,) = matmul_residual(act, act_s, wd, xp, xs, scale=0.5, tk=tk_d, k_valid=d_ff,
                                          jobs=[job_mixer_out(i)])

        if i % 2 == 0:
            zp, zs, _ = norm_linear(xp, xs, g[1], w_mi, b_qkv[j][None, :])
            op = attn_prompt(zp, attn_sinks[j], batch=batch, seq=seq, n_heads=n_heads, n_kv=n_kv)
            kn = zs[:, hq:hq + hk]
            vn = zs[:, hq + hk:]
            os_ = attn_sample(zs[:, :hq].reshape(ms, n_heads, HEAD_DIM),
                              cache_k[j].reshape(ms, wb, hk), cache_v[j].reshape(ms, wb, hk),
                              kn[:, None, :], vn[:, None, :], attn_sinks[j][:, None],
                              n_kv=n_kv)
            os_ = os_.reshape(ms, hq).astype(BF16)
            xp, xs, _ = matmul_residual(op, os_, w_mo, xp, xs, b_o[j][None, :], scale=1.0, tk=hq)
            zp3 = zp.reshape(batch, seq, qkv_dim)
            wp_ = min(WINDOW, seq)
            nkp.append(zp3[:, seq - wp_:, hq:hq + hk].reshape(batch, wp_, n_kv, HEAD_DIM))
            nvp.append(zp3[:, seq - wp_:, hq + hk:].reshape(batch, wp_, n_kv, HEAD_DIM))
            kk = jnp.concatenate([cache_k[j], kn.reshape(ms, 1, n_kv, HEAD_DIM)], axis=1)
            vv = jnp.concatenate([cache_v[j], vn.reshape(ms, 1, n_kv, HEAD_DIM)], axis=1)
            nks.append(kk[:, -wb:])
            nvs.append(vv[:, -wb:])
        else:
            (tp, cup, _), (_, cus, gbs), _ = conv_mixer_in(xp, xs, g[1], w_mi, conv_w[j], seq=seq)
            st = state_conv[j]
            ts = conv_mix_sample(gbs, cus, st[:, 0], st[:, 1], conv_w[j])
            xp, xs, _ = matmul_residual(tp, ts, w_mo, xp, xs, scale=1.0, tk=d)
            ncp.append(cup.reshape(batch, seq, d)[:, seq - (CONV_WIDTH - 1):])
            ncs.append(jnp.concatenate([st, cus[:, None, :]], axis=1)[:, -(CONV_WIDTH - 1):])

        next_up = [] if last else [job_gate(i + 1, 0), job_up(i + 1, 0)]
        act, act_s, (wd, w_pg, *wg_wu) = swiglu_up(xp, xs, g[2], wg_next, wu_next, d_ff=d_ff,
                                                   jobs=[job_down(i, 1), job_ple_gate(i)] + next_up)
        xp, xs, _ = matmul_residual(act, act_s, wd, xp, xs, scale=0.5, tk=tk_d, k_valid=d_ff)
        if not last:
            wg, wu = wg_wu

        xp, xs, _ = ple(xp, xs, g[3], w_pg, pp[i], ps[i], wpp[i])

    gf = final_norm_gain[None, :]
    y_prompt = rmsnorm(xp, gf, tm=_tile(mp, 256)).reshape(batch, seq, d)
    y_sample = rmsnorm(xs, gf, tm=ms).reshape(nseq, dec_seq, d)
    return (y_prompt, y_sample, jnp.stack(nkp), jnp.stack(nvp), jnp.stack(nks), jnp.stack(nvs),
            jnp.stack(ncp), jnp.stack(ncs))
```
